```python
import math, functools
import jax, jax.numpy as jnp
from jax import lax
import numpy as np

D_MODEL = 4096
BATCH = 4
SEQ = 2048
DEPTH = 2
DEC_BATCH = 128
DEC_SEQ = 8
PAST_LEN = 16384
PAGE_SIZE = 128

N_MIX_LAYERS = (DEPTH + 1) // 2
N_POOL_LAYERS = DEPTH // 2
RMS_EPS = 1e-6

MLA_HEADS = 16
QK_NOPE = 128
QK_ROPE = 64
V_DIM = 128
Q_LORA = 1024
KV_LORA = 512
ROPE_THETA = 10000.0
ATTN_SCALE = (QK_NOPE + QK_ROPE) ** -0.5
Q_BLOCK = 128

SSM_D_INNER = 2048
SSM_HEAD_DIM = 64
SSM_HEADS = SSM_D_INNER // SSM_HEAD_DIM
SSM_GROUPS = 8
SSM_D_STATE = 128
CONV_W = 4
CONV_DIM = SSM_D_INNER + 2 * SSM_GROUPS * SSM_D_STATE
SSD_CHUNK = 128
DT_MIN = 1e-3
DT_MAX = 1e-1
DT_PROJ_SCALE = 0.1

IN_SIZES = (Q_LORA, KV_LORA, QK_ROPE, SSM_D_INNER, CONV_DIM, SSM_HEADS)
IN_DIM = sum(IN_SIZES)
IN_SPLITS = tuple(int(v) for v in np.cumsum(IN_SIZES)[:-1])
MIX_OUT = MLA_HEADS * V_DIM + SSM_D_INNER

POOL_WINDOWS = (2, 4, 8, 16)
POOL_GROUPS = len(POOL_WINDOWS)
POOL_GROUP_DIM = D_MODEL // POOL_GROUPS
POOL_HIST = max(POOL_WINDOWS) - 1

D_FF = ((8 * D_MODEL + 3 * 256 - 1) // (3 * 256)) * 256

kernel_name = 'mla_ssd_pool_hybrid_decode_step'


def rms_norm(x, g):
    xf = x.astype(jnp.float32)
    y = xf * lax.rsqrt(jnp.mean(xf * xf, axis=-1, keepdims=True) + RMS_EPS)
    return (y * g.astype(jnp.float32)).astype(x.dtype)


def rope_tables(pos):
    inv_freq = 1.0 / (ROPE_THETA ** (jnp.arange(0, QK_ROPE, 2, dtype=jnp.float32) / QK_ROPE))
    ang = pos.astype(jnp.float32)[:, None] * inv_freq[None, :]
    return jnp.cos(ang), jnp.sin(ang)


def apply_rope(x, cos, sin):
    x1, x2 = jnp.split(x.astype(jnp.float32), 2, axis=-1)
    return jnp.concatenate([x1 * cos - x2 * sin, x1 * sin + x2 * cos], axis=-1).astype(x.dtype)


def mla_scores(q_lat, q_pe, k_lat, k_pe):
    s = jnp.einsum('bqhc,bkc->bhqk', q_lat, k_lat, preferred_element_type=jnp.float32)
    s = s + jnp.einsum('bqhr,bkr->bhqk', q_pe, k_pe, preferred_element_type=jnp.float32)
    return s * ATTN_SCALE


def mla_attend_causal(q_lat, q_pe, c_kv, k_pe):
    b, l, h, c = q_lat.shape
    nb = l // Q_BLOCK
    qb = q_lat.reshape(b, nb, Q_BLOCK, h, c).swapaxes(0, 1)
    pb = q_pe.reshape(b, nb, Q_BLOCK, h, QK_ROPE).swapaxes(0, 1)
    starts = jnp.arange(nb) * Q_BLOCK
    kpos = jnp.arange(l)
    v = c_kv.astype(jnp.float32)

    def block(args):
        ql, qp, st = args
        s = mla_scores(ql, qp, c_kv, k_pe)
        qpos = st + jnp.arange(Q_BLOCK)
        s = jnp.where(kpos[None, :] <= qpos[:, None], s, -jnp.inf)
        p = jax.nn.softmax(s, axis=-1)
        return jnp.einsum('bhqk,bkc->bqhc', p, v)

    o = lax.map(block, (qb, pb, starts))
    return o.swapaxes(0, 1).reshape(b, l, h, c)


def online_merge(m, l, acc, s, v):
    m_new = jnp.maximum(m, jnp.max(s, axis=-1))
    corr = jnp.exp(m - m_new)
    p = jnp.exp(s - m_new[..., None])
    l_new = l * corr + jnp.sum(p, axis=-1)
    acc_new = acc * corr[..., None] + jnp.einsum('bhqk,bkc->bhqc', p, v.astype(jnp.float32))
    return m_new, l_new, acc_new


def mla_attend_paged(q_lat, q_pe, c_kv, k_pe, cache_lat, cache_rope, page_table, li):
    b, t, h, _ = q_lat.shape

    def page_step(carry, phys):
        k_lat = cache_lat[li, phys]
        k_rot = cache_rope[li, phys]
        s = mla_scores(q_lat, q_pe, k_lat, k_rot)
        return online_merge(*carry, s, k_lat), None

    init = (jnp.full((b, h, t), -1e30, jnp.float32),
            jnp.zeros((b, h, t), jnp.float32),
            jnp.zeros((b, h, t, KV_LORA), jnp.float32))
    carry, _ = lax.scan(page_step, init, page_table.T)
    s_self = mla_scores(q_lat, q_pe, c_kv, k_pe)
    causal = jnp.tril(jnp.ones((t, t), bool))
    s_self = jnp.where(causal, s_self, -jnp.inf)
    m, l, acc = online_merge(*carry, s_self, c_kv)
    return (acc / l[..., None]).transpose(0, 2, 1, 3)


def ssd_chunked(x, dt, a, bm, cm, h0):
    b, l, nh, p = x.shape
    n = bm.shape[-1]
    r = nh // SSM_GROUPS
    q = min(SSD_CHUNK, l)
    nc = -(-l // q)
    pad = nc * q - l
    xf = x.astype(jnp.float32)
    bf = bm.astype(jnp.float32)
    cf = cm.astype(jnp.float32)
    if pad:
        xf = jnp.pad(xf, ((0, 0), (0, pad), (0, 0), (0, 0)))
        bf = jnp.pad(bf, ((0, 0), (0, pad), (0, 0), (0, 0)))
        cf = jnp.pad(cf, ((0, 0), (0, pad), (0, 0), (0, 0)))
        dt = jnp.pad(dt, ((0, 0), (0, pad), (0, 0)))
    xf = xf.reshape(b, nc, q, SSM_GROUPS, r, p)
    bf = bf.reshape(b, nc, q, SSM_GROUPS, n)
    cf = cf.reshape(b, nc, q, SSM_GROUPS, n)
    dtc = dt.reshape(b, nc, q, SSM_GROUPS, r)
    a_cum = jnp.cumsum(dtc * a.reshape(SSM_GROUPS, r), axis=2)
    causal = jnp.tril(jnp.ones((q, q), bool))[:, :, None, None]
    seg = jnp.where(causal, a_cum[:, :, :, None] - a_cum[:, :, None, :], -jnp.inf)
    cb = jnp.einsum('bcign,bcjgn->bcijg', cf, bf)
    w = cb[..., None] * jnp.exp(seg) * dtc[:, :, None]
    y_diag = jnp.einsum('bcijgr,bcjgrp->bcigrp', w, xf)
    decay = jnp.exp(a_cum[:, :, -1:] - a_cum) * dtc
    states = jnp.einsum('bcjgn,bcjgrp->bcgrpn', bf, xf * decay[..., None])
    chunk_decay = jnp.exp(a_cum[:, :, -1])

    def chunk_step(h, inp):
        st, dec = inp
        return h * dec[..., None, None] + st, h

    h0g = h0.astype(jnp.float32).reshape(b, SSM_GROUPS, r, p, n)
    h_last, h_in = lax.scan(chunk_step, h0g, (states.swapaxes(0, 1), chunk_decay.swapaxes(0, 1)))
    y_off = jnp.einsum('bcign,cbgrpn->bcigrp', cf, h_in) * jnp.exp(a_cum)[..., None]
    y = (y_diag + y_off).reshape(b, nc * q, nh, p)[:, :l]
    return y, h_last.reshape(b, nh, p, n)


def hybrid_mixer(u, pos, conv_hist, ssm_h0, attend, w_in, q_norm_g, kv_norm_g, w_uq, w_ukv,
                 conv_w, conv_b, dt_bias, a_log, d_skip, ssm_norm_g, w_out):
    b, l, _ = u.shape
    proj = jnp.einsum('bld,de->ble', u, w_in)
    c_q, c_kv, k_pe, z, xbc_raw, dt_raw = jnp.split(proj, IN_SPLITS, axis=-1)
    cos, sin = rope_tables(pos)
    q = jnp.einsum('blc,chd->blhd', rms_norm(c_q, q_norm_g), w_uq)
    q_pe = apply_rope(q[..., QK_NOPE:], cos[:, None, :], sin[:, None, :])
    q_lat = jnp.einsum('blhd,chd->blhc', q[..., :QK_NOPE], w_ukv[..., :QK_NOPE])
    c_kv = rms_norm(c_kv, kv_norm_g)
    k_pe = apply_rope(k_pe, cos, sin)
    o_lat = attend(q_lat, q_pe, c_kv, k_pe)
    o_mla = jnp.einsum('blhc,chv->blhv', o_lat.astype(u.dtype), w_ukv[..., QK_NOPE:])
    o_mla = o_mla.reshape(b, l, MLA_HEADS * V_DIM)
    xpad = jnp.concatenate([conv_hist.astype(xbc_raw.dtype), xbc_raw], axis=1)
    xbc = lax.conv_general_dilated(xpad, conv_w[:, None, :].astype(xpad.dtype), (1,), 'VALID',
                                   dimension_numbers=('NWC', 'WIO', 'NWC'),
                                   feature_group_count=CONV_DIM)
    xbc = jax.nn.silu(xbc + conv_b)
    xs, bm, cm = jnp.split(xbc, [SSM_D_INNER, SSM_D_INNER + SSM_GROUPS * SSM_D_STATE], axis=-1)
    xs = xs.reshape(b, l, SSM_HEADS, SSM_HEAD_DIM)
    dt = jax.nn.softplus(dt_raw.astype(jnp.float32) + dt_bias.astype(jnp.float32))
    a = -jnp.exp(a_log.astype(jnp.float32))
    y, h_new = ssd_chunked(xs, dt, a, bm.reshape(b, l, SSM_GROUPS, SSM_D_STATE),
                           cm.reshape(b, l, SSM_GROUPS, SSM_D_STATE), ssm_h0)
    y = y + d_skip.astype(jnp.float32)[:, None] * xs.astype(jnp.float32)
    gated = (y.reshape(b, l, SSM_D_INNER) * jax.nn.silu(z.astype(jnp.float32))).astype(u.dtype)
    y_ssm = rms_norm(gated, ssm_norm_g)
    out = jnp.einsum('ble,ed->bld', jnp.concatenate([o_mla, y_ssm], axis=-1), w_out)
    return out, c_kv, k_pe, xpad[:, -(CONV_W - 1):], h_new.astype(u.dtype)


def pool_mixer(u, hist, w_pool, pool_scale, pos0):
    b, l, _ = u.shape
    full = jnp.concatenate([hist.astype(u.dtype), u], axis=1)
    csum = jnp.pad(jnp.cumsum(full.astype(jnp.float32), axis=1), ((0, 0), (1, 0), (0, 0)))
    count_pos = (pos0 + jnp.arange(l) + 1).astype(jnp.float32)[None, :, None]
    groups = []
    for gi, win in enumerate(POOL_WINDOWS):
        ch = slice(gi * POOL_GROUP_DIM, (gi + 1) * POOL_GROUP_DIM)
        wsum = (csum[:, POOL_HIST + 1:POOL_HIST + 1 + l, ch]
                - csum[:, POOL_HIST + 1 - win:POOL_HIST + 1 - win + l, ch])
        mean = wsum / jnp.minimum(count_pos, float(win))
        groups.append(mean - u[..., ch].astype(jnp.float32))
    mix = jnp.stack(groups, axis=2).astype(u.dtype)
    y = jnp.einsum('blgc,gcd->blgd', mix, w_pool).reshape(b, l, D_MODEL) * pool_scale
    return y, full[:, -POOL_HIST:]


def swiglu(u, w_gate, w_up, w_down):
    return jnp.einsum('blf,fd->bld', jax.nn.silu(jnp.einsum('bld,df->blf', u, w_gate))
                      * jnp.einsum('bld,df->blf', u, w_up), w_down)


def setup_inputs(seed: int = 0) -> dict:
    key = jax.random.key(seed)
    keys = iter(jax.random.split(key, 40))

    def normal(shape, scale):
        return scale * jax.random.normal(next(keys), shape, jnp.float32)

    def gain(shape, scale=0.02):
        return 1.0 + scale * jax.random.normal(next(keys), shape, jnp.float32)

    n_pages = PAST_LEN // PAGE_SIZE
    n_used = DEC_BATCH * n_pages
    n_phys = n_used + n_used // 4
    page_table = jax.random.permutation(next(keys), n_phys)[:n_used].reshape(DEC_BATCH, n_pages).astype(jnp.int32)
    w_in = normal((N_MIX_LAYERS, D_MODEL, IN_DIM), D_MODEL ** -0.5)
    w_in = w_in.at[..., -SSM_HEADS:].multiply(DT_PROJ_SCALE)
    dt0 = jnp.exp(jax.random.uniform(next(keys), (N_MIX_LAYERS, SSM_HEADS), jnp.float32,
                                     math.log(DT_MIN), math.log(DT_MAX)))
    dt_bias = dt0 + jnp.log(-jnp.expm1(-dt0))
    a_log = jnp.log(jax.random.uniform(next(keys), (N_MIX_LAYERS, SSM_HEADS), jnp.float32, 1.0, 16.0))
    return {
        'x_prompt': normal((BATCH, SEQ, D_MODEL), 1.0),
        'x_sample': normal((DEC_BATCH, DEC_SEQ, D_MODEL), 1.0),
        'cache_mla_latent': normal((N_MIX_LAYERS, n_phys, PAGE_SIZE, KV_LORA), 1.0),
        'cache_mla_rope': normal((N_MIX_LAYERS, n_phys, PAGE_SIZE, QK_ROPE), 1.0),
        'state_ssm_conv': normal((N_MIX_LAYERS, DEC_BATCH, CONV_W - 1, CONV_DIM), 1.0),
        'state_ssm': normal((N_MIX_LAYERS, DEC_BATCH, SSM_HEADS, SSM_HEAD_DIM, SSM_D_STATE), 0.1),
        'state_pool': normal((N_POOL_LAYERS, DEC_BATCH, POOL_HIST, D_MODEL), 1.0),
        'page_table': page_table,
        'norm_mix_g': gain((DEPTH, D_MODEL)),
        'norm_ffn_g': gain((DEPTH, D_MODEL)),
        'norm_final_g': gain((D_MODEL,)),
        'w_in': w_in,
        'q_norm_g': gain((N_MIX_LAYERS, Q_LORA)),
        'kv_norm_g': gain((N_MIX_LAYERS, KV_LORA)),
        'w_uq': normal((N_MIX_LAYERS, Q_LORA, MLA_HEADS, QK_NOPE + QK_ROPE), Q_LORA ** -0.5),
        'w_ukv': normal((N_MIX_LAYERS, KV_LORA, MLA_HEADS, QK_NOPE + V_DIM), KV_LORA ** -0.5),
        'conv_w': normal((N_MIX_LAYERS, CONV_W, CONV_DIM), CONV_W ** -0.5),
        'conv_b': normal((N_MIX_LAYERS, CONV_DIM), 0.02),
        'dt_bias': dt_bias,
        'a_log': a_log,
        'd_skip': gain((N_MIX_LAYERS, SSM_HEADS), 0.1),
        'ssm_norm_g': gain((N_MIX_LAYERS, SSM_D_INNER)),
        'w_out': normal((N_MIX_LAYERS, MIX_OUT, D_MODEL), MIX_OUT ** -0.5),
        'w_pool': normal((N_POOL_LAYERS, POOL_GROUPS, POOL_GROUP_DIM, POOL_GROUP_DIM), POOL_GROUP_DIM ** -0.5),
        'pool_scale': gain((N_POOL_LAYERS, D_MODEL), 0.1),
        'w_gate': normal((DEPTH, D_MODEL, D_FF), D_MODEL ** -0.5),
        'w_up': normal((DEPTH, D_MODEL, D_FF), D_MODEL ** -0.5),
        'w_down': normal((DEPTH, D_FF, D_MODEL), D_FF ** -0.5),
    }


def reference(x_prompt, x_sample, cache_mla_latent, cache_mla_rope, state_ssm_conv, state_ssm, state_pool,
              page_table, norm_mix_g, norm_ffn_g, norm_final_g, w_in, q_norm_g, kv_norm_g, w_uq, w_ukv,
              conv_w, conv_b, dt_bias, a_log, d_skip, ssm_norm_g, w_out, w_pool, pool_scale,
              w_gate, w_up, w_down):
    bp, lp = x_prompt.shape[0], x_prompt.shape[1]
    bs, ls = x_sample.shape[0], x_sample.shape[1]
    pos_p = jnp.arange(lp)
    pos_s = PAST_LEN + jnp.arange(ls)
    xp, xs = x_prompt, x_sample
    lat_p, rope_p, lat_s, rope_s = [], [], [], []
    conv_p, conv_s, ssm_p, ssm_s, pool_p, pool_s = [], [], [], [], [], []
    for layer in range(DEPTH):
        if layer % 2 == 0:
            li = layer // 2
            mix_w = (w_in[li], q_norm_g[li], kv_norm_g[li], w_uq[li], w_ukv[li], conv_w[li], conv_b[li],
                     dt_bias[li], a_log[li], d_skip[li], ssm_norm_g[li], w_out[li])
            up = rms_norm(xp, norm_mix_g[layer])
            yp, ckv, kpe, cst, hst = hybrid_mixer(
                up, pos_p, jnp.zeros((bp, CONV_W - 1, CONV_DIM), up.dtype),
                jnp.zeros((bp, SSM_HEADS, SSM_HEAD_DIM, SSM_D_STATE), jnp.float32),
                mla_attend_causal, *mix_w)
            lat_p.append(ckv); rope_p.append(kpe); conv_p.append(cst); ssm_p.append(hst)
            us = rms_norm(xs, norm_mix_g[layer])
            attend_s = functools.partial(mla_attend_paged, cache_lat=cache_mla_latent,
                                         cache_rope=cache_mla_rope, page_table=page_table, li=li)
            ys, ckv, kpe, cst, hst = hybrid_mixer(us, pos_s, state_ssm_conv[li], state_ssm[li], attend_s, *mix_w)
            lat_s.append(ckv); rope_s.append(kpe); conv_s.append(cst); ssm_s.append(hst)
        else:
            pi = layer // 2
            up = rms_norm(xp, norm_mix_g[layer])
            yp, hist = pool_mixer(up, jnp.zeros((bp, POOL_HIST, D_MODEL), up.dtype), w_pool[pi], pool_scale[pi], 0)
            pool_p.append(hist)
            us = rms_norm(xs, norm_mix_g[layer])
            ys, hist = pool_mixer(us, state_pool[pi], w_pool[pi], pool_scale[pi], PAST_LEN)
            pool_s.append(hist)
        xp = xp + yp
        xs = xs + ys
        xp = xp + swiglu(rms_norm(xp, norm_ffn_g[layer]), w_gate[layer], w_up[layer], w_down[layer])
        xs = xs + swiglu(rms_norm(xs, norm_ffn_g[layer]), w_gate[layer], w_up[layer], w_down[layer])
    y_prompt = rms_norm(xp, norm_final_g)
    y_sample = rms_norm(xs, norm_final_g)
    return (y_prompt, y_sample,
            jnp.stack(lat_p), jnp.stack(rope_p), jnp.stack(lat_s), jnp.stack(rope_s),
            jnp.stack(conv_p), jnp.stack(conv_s), jnp.stack(ssm_p), jnp.stack(ssm_s),
            jnp.stack(pool_p), jnp.stack(pool_s))
```

```python
import functools
import math

import jax
import jax.numpy as jnp
import numpy as np
from jax import lax
from jax.experimental import pallas as pl
from jax.experimental.pallas import tpu as pltpu

f32 = jnp.float32
bf16 = jnp.bfloat16

RMS_EPS = 1e-6
ROPE_THETA = 10000.0
PAST_LEN = 16384
PAGE_SIZE = 128
MLA_HEADS = 16
QK_NOPE = 128
QK_ROPE = 64
V_DIM = 128
Q_LORA = 1024
KV_LORA = 512
ATTN_SCALE = (QK_NOPE + QK_ROPE) ** -0.5
SSM_D_INNER = 2048
SSM_HEAD_DIM = 64
SSM_HEADS = 32
SSM_GROUPS = 8
SSM_D_STATE = 128
CONV_W = 4
CONV_DIM = SSM_D_INNER + 2 * SSM_GROUPS * SSM_D_STATE
SSD_CHUNK = 128
POOL_WINDOWS = (2, 4, 8, 16)
POOL_HIST = 15

LANES = 128
SUBLANES = 8
GROUP_P = SSM_D_INNER // SSM_GROUPS
HEADS_PER_GROUP = SSM_HEADS // SSM_GROUPS
ROPE_PAD = LANES
HALO = 16
CONV_HALO = SUBLANES

COL_XBC = 0
COL_Z = CONV_DIM
COL_Q = COL_Z + SSM_D_INNER
COL_KV = COL_Q + Q_LORA
COL_KR = COL_KV + KV_LORA
COL_KRR = COL_KR + ROPE_PAD
COL_DT = COL_KRR + ROPE_PAD
PROJ_W = COL_DT + 2 * LANES


def _dot(a, b):
    return jnp.dot(a, b, preferred_element_type=f32)


def _dot_nt(a, b):
    return lax.dot_general(a, b, (((1,), (1,)), ((), ())), preferred_element_type=f32)


def _sigmoid(x):
    return 1.0 / (1.0 + jnp.exp(-x))


def _split3(x):
    hi = x.astype(bf16).astype(f32)
    r1 = x - hi
    mid = r1.astype(bf16).astype(f32)
    lo = (r1 - mid).astype(bf16).astype(f32)
    return hi, mid, lo


def _cparams(sem, vmem_mib):
    return pltpu.CompilerParams(dimension_semantics=sem, vmem_limit_bytes=vmem_mib << 20)


def _norm_kernel(x_ref, g_ref, o_ref):
    x = x_ref[...].astype(f32)
    ms = jnp.mean(x * x, axis=-1, keepdims=True)
    o_ref[...] = (x * lax.rsqrt(ms + RMS_EPS) * g_ref[...]).astype(o_ref.dtype)


def rmsnorm(x, g, *, tm, out_dtype, rows=None, row_blk0=0, col_blk=0, width=None):
    rows = x.shape[0] if rows is None else rows
    width = x.shape[1] if width is None else width
    return pl.pallas_call(
        _norm_kernel,
        grid=(rows // tm,),
        in_specs=[pl.BlockSpec((tm, width), lambda i: (i + row_blk0, col_blk)),
                  pl.BlockSpec((1, width), lambda i: (0, 0))],
        out_specs=pl.BlockSpec((tm, width), lambda i: (i, 0)),
        out_shape=jax.ShapeDtypeStruct((rows, width), out_dtype),
        compiler_params=_cparams(("parallel",), 40),
        name="rmsnorm",
    )(x, g.reshape(1, width).astype(f32))


def _mm_kernel(a_ref, w_ref, o_ref):
    o_ref[...] = _dot(a_ref[...], w_ref[...]).astype(o_ref.dtype)


def _mm_res_kernel(a_ref, w_ref, r_ref, o_ref):
    o_ref[...] = (_dot(a_ref[...], w_ref[...]) + r_ref[...]).astype(o_ref.dtype)


def matmul(a, w, *, tm, tn, out_dtype, residual=None, name="matmul"):
    m, k = a.shape
    n = w.shape[1]
    in_specs = [pl.BlockSpec((tm, k), lambda i, j: (i, 0)),
                pl.BlockSpec((k, tn), lambda i, j: (0, j))]
    args = [a, w]
    body = _mm_kernel
    if residual is not None:
        in_specs.append(pl.BlockSpec((tm, tn), lambda i, j: (i, j)))
        args.append(residual)
        body = _mm_res_kernel
    return pl.pallas_call(
        body,
        grid=(m // tm, n // tn),
        in_specs=in_specs,
        out_specs=pl.BlockSpec((tm, tn), lambda i, j: (i, j)),
        out_shape=jax.ShapeDtypeStruct((m, n), out_dtype),
        compiler_params=_cparams(("parallel", "arbitrary"), 56),
        name=name,
    )(*args)


def _ffn_up_kernel(a_ref, wg_ref, wu_ref, o_ref):
    a = a_ref[...]
    g = _dot(a, wg_ref[...])
    u = _dot(a, wu_ref[...])
    o_ref[...] = (g * _sigmoid(g) * u).astype(o_ref.dtype)


def ffn_up(a, wg, wu, *, tm, tn):
    m, k = a.shape
    n = wg.shape[1]
    return pl.pallas_call(
        _ffn_up_kernel,
        grid=(m // tm, n // tn),
        in_specs=[pl.BlockSpec((tm, k), lambda i, j: (i, 0)),
                  pl.BlockSpec((k, tn), lambda i, j: (0, j)),
                  pl.BlockSpec((k, tn), lambda i, j: (0, j))],
        out_specs=pl.BlockSpec((tm, tn), lambda i, j: (i, j)),
        out_shape=jax.ShapeDtypeStruct((m, n), bf16),
        compiler_params=_cparams(("parallel", "arbitrary"), 56),
        name="ffn_up",
    )(a, wg, wu)


def _pool_mm_kernel(a_ref, w_ref, s_ref, r_ref, o_ref):
    o_ref[...] = _dot(a_ref[...], w_ref[0]) * s_ref[...] + r_ref[...]


def pool_matmul(mix, w, scale, residual, *, tm, tn):
    m, d = mix.shape
    ng, gd, _ = w.shape
    nj = gd // tn
    return pl.pallas_call(
        _pool_mm_kernel,
        grid=(m // tm, ng, nj),
        in_specs=[pl.BlockSpec((tm, gd), lambda i, g, j: (i, g)),
                  pl.BlockSpec((1, gd, tn), lambda i, g, j: (g, 0, j)),
                  pl.BlockSpec((1, tn), lambda i, g, j: (0, g * nj + j)),
                  pl.BlockSpec((tm, tn), lambda i, g, j: (i, g * nj + j))],
        out_specs=pl.BlockSpec((tm, tn), lambda i, g, j: (i, g * nj + j)),
        out_shape=jax.ShapeDtypeStruct((m, d), f32),
        compiler_params=_cparams(("parallel", "arbitrary", "arbitrary"), 40),
        name="pool_matmul",
    )(mix, w, scale.reshape(1, d).astype(f32), residual)


def _kv_prep_kernel(kv_ref, kr_ref, krr_ref, cos_ref, sin_ref, g_ref, ckv_ref, ckvb_ref, kpe_ref, kpeb_ref):
    x = kv_ref[...]
    ms = jnp.mean(x * x, axis=-1, keepdims=True)
    ckv = x * lax.rsqrt(ms + RMS_EPS) * g_ref[...]
    ckv_ref[...] = ckv
    ckvb_ref[...] = ckv.astype(bf16)
    kpe = kr_ref[...] * cos_ref[...] + krr_ref[...] * sin_ref[...]
    kpe_ref[...] = kpe
    kpeb_ref[...] = kpe.astype(bf16)


def kv_prep(proj, cos_t, sin_t, kv_norm_g, *, tm):
    m = proj.shape[0]
    row = lambda w, cb: pl.BlockSpec((tm, w), lambda i: (i, cb))
    return pl.pallas_call(
        _kv_prep_kernel,
        grid=(m // tm,),
        in_specs=[row(KV_LORA, COL_KV // KV_LORA), row(ROPE_PAD, COL_KR // ROPE_PAD),
                  row(ROPE_PAD, COL_KRR // ROPE_PAD), row(ROPE_PAD, 0), row(ROPE_PAD, 0),
                  pl.BlockSpec((1, KV_LORA), lambda i: (0, 0))],
        out_specs=[row(KV_LORA, 0), row(KV_LORA, 0), row(ROPE_PAD, 0), row(ROPE_PAD, 0)],
        out_shape=[jax.ShapeDtypeStruct((m, KV_LORA), f32), jax.ShapeDtypeStruct((m, KV_LORA), bf16),
                   jax.ShapeDtypeStruct((m, ROPE_PAD), f32), jax.ShapeDtypeStruct((m, ROPE_PAD), bf16)],
        compiler_params=_cparams(("parallel",), 32),
        name="kv_prep",
    )(proj, proj, proj, cos_t, sin_t, kv_norm_g.reshape(1, KV_LORA).astype(f32))


def _q_prep_kernel(cq_ref, g_ref, wn_ref, wr_ref, wrr_ref, wuk_ref, cos_ref, sin_ref, qlat_ref, qpe_ref, cqn_sc):
    @pl.when(pl.program_id(1) == 0)
    def _():
        x = cq_ref[...]
        ms = jnp.mean(x * x, axis=-1, keepdims=True)
        cqn_sc[...] = (x * lax.rsqrt(ms + RMS_EPS) * g_ref[...]).astype(bf16)

    cqn = cqn_sc[...]
    q_nope = _dot(cqn, wn_ref[0]).astype(bf16)
    qlat_ref[0] = _dot(q_nope, wuk_ref[0]).astype(qlat_ref.dtype)
    q_r = _dot(cqn, wr_ref[0])
    q_rr = _dot(cqn, wrr_ref[0])
    qpe_ref[0] = (q_r * cos_ref[...] + q_rr * sin_ref[...]).astype(qpe_ref.dtype)


def q_prep(proj, cos_t, sin_t, q_norm_g, wn, wr, wrr, wuk, *, rows, row_blk0, tm, out_dtype):
    nh = wn.shape[0]
    hw = lambda a, b: pl.BlockSpec((1, a, b), lambda i, h: (h, 0, 0))
    return pl.pallas_call(
        _q_prep_kernel,
        grid=(rows // tm, nh),
        in_specs=[pl.BlockSpec((tm, Q_LORA), lambda i, h: (i + row_blk0, COL_Q // Q_LORA)),
                  pl.BlockSpec((1, Q_LORA), lambda i, h: (0, 0)),
                  hw(Q_LORA, QK_NOPE), hw(Q_LORA, ROPE_PAD), hw(Q_LORA, ROPE_PAD), hw(QK_NOPE, KV_LORA),
                  pl.BlockSpec((tm, ROPE_PAD), lambda i, h: (i + row_blk0, 0)),
                  pl.BlockSpec((tm, ROPE_PAD), lambda i, h: (i + row_blk0, 0))],
        out_specs=[pl.BlockSpec((1, tm, KV_LORA), lambda i, h: (h, i, 0)),
                   pl.BlockSpec((1, tm, ROPE_PAD), lambda i, h: (h, i, 0))],
        out_shape=[jax.ShapeDtypeStruct((nh, rows, KV_LORA), out_dtype),
                   jax.ShapeDtypeStruct((nh, rows, ROPE_PAD), out_dtype)],
        scratch_shapes=[pltpu.VMEM((tm, Q_LORA), bf16)],
        compiler_params=_cparams(("parallel", "arbitrary"), 32),
        name="q_prep",
    )(proj, q_norm_g.reshape(1, Q_LORA).astype(f32), wn, wr, wrr, wuk, cos_t, sin_t)


def _softmax_step(s, v, m_sc, l_sc, acc_sc):
    m_prev = m_sc[...]
    m_new = jnp.maximum(m_prev, jnp.max(s, axis=-1, keepdims=True))
    alpha = jnp.exp(m_prev - m_new)
    p = jnp.exp(s - m_new)
    l_sc[...] = alpha * l_sc[...] + jnp.sum(p, axis=-1, keepdims=True)
    acc_sc[...] = alpha * acc_sc[...] + _dot(p.astype(bf16), v)
    m_sc[...] = m_new


def _attn_prompt_kernel(qlat_ref, qpe_ref, klat_ref, kpe_ref, wuv_ref, o_ref, m_sc, l_sc, acc_sc, *, tq, tk):
    nh = qlat_ref.shape[0]
    rows = nh * tq
    qb = pl.program_id(1)
    m_sc[...] = jnp.full(m_sc.shape, -1e30, f32)
    l_sc[...] = jnp.zeros(l_sc.shape, f32)
    acc_sc[...] = jnp.zeros(acc_sc.shape, f32)

    def step(kb, masked):
        k = klat_ref[0, pl.ds(pl.multiple_of(kb * tk, tk), tk), :]
        kp = kpe_ref[0, pl.ds(pl.multiple_of(kb * tk, tk), tk), :]
        q = qlat_ref[...].reshape(rows, KV_LORA)
        qp = qpe_ref[...].reshape(rows, ROPE_PAD)
        s = (_dot_nt(q, k) + _dot_nt(qp, kp)) * ATTN_SCALE
        if masked:
            qpos = qb * tq + lax.broadcasted_iota(jnp.int32, (nh, tq, tk), 1).reshape(rows, tk)
            kpos = kb * tk + lax.broadcasted_iota(jnp.int32, (rows, tk), 1)
            s = jnp.where(kpos <= qpos, s, -jnp.inf)
        _softmax_step(s, k, m_sc, l_sc, acc_sc)

    n_full = (qb * tq) // tk

    def body(kb, carry):
        step(kb, False)
        return carry

    lax.fori_loop(0, n_full, body, 0)
    step(n_full, True)

    o = (acc_sc[...] / l_sc[...]).astype(bf16).reshape(nh, tq, KV_LORA)
    for h in range(nh):
        o_ref[0, :, h * V_DIM:(h + 1) * V_DIM] = _dot(o[h], wuv_ref[h]).astype(o_ref.dtype)


def attn_prompt(qlat, qpe, klat, kpe, wuv, *, nb, seq, tq, tk):
    nh = qlat.shape[0]
    assert tk % tq == 0 and seq % tk == 0
    nqb = seq // tq
    rows = nh * tq
    kern = functools.partial(_attn_prompt_kernel, tq=tq, tk=tk)
    return pl.pallas_call(
        kern,
        grid=(nb, nqb),
        in_specs=[pl.BlockSpec((nh, tq, KV_LORA), lambda b, i: (0, b * nqb + i, 0)),
                  pl.BlockSpec((nh, tq, ROPE_PAD), lambda b, i: (0, b * nqb + i, 0)),
                  pl.BlockSpec((1, seq, KV_LORA), lambda b, i: (b, 0, 0)),
                  pl.BlockSpec((1, seq, ROPE_PAD), lambda b, i: (b, 0, 0)),
                  pl.BlockSpec((nh, KV_LORA, V_DIM), lambda b, i: (0, 0, 0))],
        out_specs=pl.BlockSpec((1, tq, nh * V_DIM), lambda b, i: (b, i, 0)),
        out_shape=jax.ShapeDtypeStruct((nb, seq, nh * V_DIM), bf16),
        scratch_shapes=[pltpu.VMEM((rows, 1), f32), pltpu.VMEM((rows, 1), f32),
                        pltpu.VMEM((rows, KV_LORA), f32)],
        compiler_params=_cparams(("parallel", "arbitrary"), 48),
        name="attn_prompt",
    )(qlat, qpe, klat, kpe, wuv)


def _attn_sample_kernel(pt_ref, qlat_ref, qpe_ref, slat_ref, spe_ref, *rest, pages_per_step, t_new):
    pp = pages_per_step
    lat_refs = rest[:pp]
    rope_refs = rest[pp:2 * pp]
    wuv_ref, o_ref, m_sc, l_sc, acc_sc, q_sc, qp_sc, k_sc, kr_sc = rest[2 * pp:]
    nh = qlat_ref.shape[0]
    rows = nh * t_new
    step = pl.program_id(1)

    @pl.when(step == 0)
    def _():
        q = qlat_ref[...].reshape(rows, KV_LORA).astype(bf16)
        qp = qpe_ref[...].reshape(rows, ROPE_PAD).astype(bf16)
        q_sc[...] = q
        qp_sc[...] = qp[:, :QK_ROPE]
        pad = 2 * SUBLANES - t_new
        ks = jnp.concatenate([slat_ref[...], jnp.zeros((pad, KV_LORA), f32)], axis=0).astype(bf16)
        kps = jnp.concatenate([spe_ref[...], jnp.zeros((pad, ROPE_PAD), f32)], axis=0).astype(bf16)
        s = (_dot_nt(q, ks) + _dot_nt(qp, kps)) * ATTN_SCALE
        tpos = lax.broadcasted_iota(jnp.int32, (nh, t_new, 2 * SUBLANES), 1).reshape(rows, 2 * SUBLANES)
        kpos = lax.broadcasted_iota(jnp.int32, (rows, 2 * SUBLANES), 1)
        s = jnp.where(kpos <= tpos, s, -jnp.inf)
        m0 = jnp.maximum(jnp.max(s, axis=-1, keepdims=True), -1e30)
        p = jnp.exp(s - m0)
        m_sc[...] = m0
        l_sc[...] = jnp.sum(p, axis=-1, keepdims=True)
        acc_sc[...] = _dot(p.astype(bf16), ks)

    for j in range(pp):
        k_sc[j * PAGE_SIZE:(j + 1) * PAGE_SIZE, :] = lat_refs[j][0, 0].astype(bf16)
        kr_sc[j * PAGE_SIZE:(j + 1) * PAGE_SIZE, :] = rope_refs[j][0, 0].astype(bf16)
    k = k_sc[...]
    s = (_dot_nt(q_sc[...], k) + _dot_nt(qp_sc[...], kr_sc[...])) * ATTN_SCALE
    _softmax_step(s, k, m_sc, l_sc, acc_sc)

    @pl.when(step == pl.num_programs(1) - 1)
    def _():
        o = (acc_sc[...] / l_sc[...]).astype(bf16).astype(f32).reshape(nh, t_new, KV_LORA)
        for h in range(nh):
            o_ref[0, :, h * V_DIM:(h + 1) * V_DIM] = _dot(o[h], wuv_ref[h].astype(f32))


def attn_sample(page_table, qlat, qpe, ckv, kpe, cache_lat, cache_rope, wuv, *, li, row0, t_new, pages_per_step):
    nh = qlat.shape[0]
    nb, n_pages = page_table.shape
    pp = pages_per_step
    assert n_pages % pp == 0 and row0 % t_new == 0
    rows = nh * t_new
    blk0 = row0 // t_new
    page = lambda width, j: pl.BlockSpec(
        (1, 1, PAGE_SIZE, width), lambda b, s, pt: (li, pt[b, s * pp + j], 0, 0))
    kern = functools.partial(_attn_sample_kernel, pages_per_step=pp, t_new=t_new)
    grid_spec = pltpu.PrefetchScalarGridSpec(
        num_scalar_prefetch=1,
        grid=(nb, n_pages // pp),
        in_specs=[pl.BlockSpec((nh, t_new, KV_LORA), lambda b, s, pt: (0, b, 0)),
                  pl.BlockSpec((nh, t_new, ROPE_PAD), lambda b, s, pt: (0, b, 0)),
                  pl.BlockSpec((t_new, KV_LORA), lambda b, s, pt: (blk0 + b, 0)),
                  pl.BlockSpec((t_new, ROPE_PAD), lambda b, s, pt: (blk0 + b, 0))]
                 + [page(KV_LORA, j) for j in range(pp)]
                 + [page(QK_ROPE, j) for j in range(pp)]
                 + [pl.BlockSpec((nh, KV_LORA, V_DIM), lambda b, s, pt: (0, 0, 0))],
        out_specs=pl.BlockSpec((1, t_new, nh * V_DIM), lambda b, s, pt: (b, 0, 0)),
        scratch_shapes=[pltpu.VMEM((rows, 1), f32), pltpu.VMEM((rows, 1), f32),
                        pltpu.VMEM((rows, KV_LORA), f32),
                        pltpu.VMEM((rows, KV_LORA), bf16), pltpu.VMEM((rows, QK_ROPE), bf16),
                        pltpu.VMEM((pp * PAGE_SIZE, KV_LORA), bf16),
                        pltpu.VMEM((pp * PAGE_SIZE, QK_ROPE), bf16)],
    )
    return pl.pallas_call(
        kern,
        grid_spec=grid_spec,
        out_shape=jax.ShapeDtypeStruct((nb, t_new, nh * V_DIM), f32),
        compiler_params=_cparams(("parallel", "arbitrary"), 32),
        name="attn_sample",
    )(page_table, qlat, qpe, ckv, kpe, *([cache_lat] * pp), *([cache_rope] * pp), wuv)


def _expand_heads(cols, n):
    q = cols.shape[0]
    lane_head = lax.broadcasted_iota(jnp.int32, (q, GROUP_P), 1) // SSM_HEAD_DIM
    out = jnp.broadcast_to(cols[:, 0:1], (q, GROUP_P))
    for r in range(1, n):
        out = jnp.where(lane_head == r, cols[:, r:r + 1], out)
    return out


def _ssd_kernel(xbc_ref, z_ref, dt_ref, hist_ref, h0_ref, cw_ref, cb_ref, dtb_ref, a_ref, dsk_ref, ng_ref,
                y_ref, hout_ref, ext_sc, h_sc, y_sc, *, q, mxu_dtype):
    c = pl.program_id(1)
    cast = lambda v: v.astype(mxu_dtype)

    @pl.when(c == 0)
    def _():
        ext_sc[0:CONV_HALO, :] = hist_ref[0]
        h_sc[...] = h0_ref[0]

    ext_sc[CONV_HALO:CONV_HALO + q, :] = xbc_ref[...]
    conv = cb_ref[...]
    for k in range(CONV_W):
        conv = conv + ext_sc[pl.ds(CONV_HALO - (CONV_W - 1) + k, q), :] * cw_ref[k:k + 1, :]
    ext_sc[0:CONV_HALO, :] = ext_sc[q:q + CONV_HALO, :]
    act = conv * _sigmoid(conv)

    dt_in = dt_ref[...] + dtb_ref[...]
    dt = jnp.maximum(dt_in, 0.0) + jnp.log1p(jnp.exp(-jnp.abs(dt_in)))
    dta = dt * a_ref[...]
    ri = lax.broadcasted_iota(jnp.int32, (q, q), 0)
    ci = lax.broadcasted_iota(jnp.int32, (q, q), 1)
    tril = ci <= ri
    trif = tril.astype(f32)
    eye_l = (lax.broadcasted_iota(jnp.int32, (LANES, LANES), 0)
             == lax.broadcasted_iota(jnp.int32, (LANES, LANES), 1)).astype(f32)
    eye_p = (lax.broadcasted_iota(jnp.int32, (GROUP_P, GROUP_P), 0)
             == lax.broadcasted_iota(jnp.int32, (GROUP_P, GROUP_P), 1)).astype(mxu_dtype)
    a_cum = sum(_dot(trif, part) for part in _split3(dta))
    a_cum_t = sum(_dot_nt(eye_l, part) for part in _split3(a_cum))
    dt_t = sum(_dot_nt(eye_l, part) for part in _split3(dt))
    a_last = a_cum[q - 1:q, :]
    decay = jnp.exp(a_last - a_cum) * dt
    e_acum = jnp.exp(a_cum)
    e_last_t = jnp.exp(a_cum_t[:, q - 1:q])
    lane_head = lax.broadcasted_iota(jnp.int32, (q, GROUP_P), 1) // SSM_HEAD_DIM

    for g in range(SSM_GROUPS):
        h_lo = g * HEADS_PER_GROUP
        bg = cast(act[:, SSM_D_INNER + g * SSM_D_STATE:SSM_D_INNER + (g + 1) * SSM_D_STATE])
        cg = cast(act[:, SSM_D_INNER + (SSM_GROUPS + g) * SSM_D_STATE:
                      SSM_D_INNER + (SSM_GROUPS + g + 1) * SSM_D_STATE])
        xg = act[:, g * GROUP_P:(g + 1) * GROUP_P]
        xg_m = cast(xg)
        cb = _dot_nt(cg, bg)
        hg = h_sc[g]
        y_off = _dot_nt(cg, cast(hg))
        y_diag = jnp.zeros((q, GROUP_P), f32)
        for r in range(HEADS_PER_GROUP):
            h = h_lo + r
            seg = a_cum[:, h:h + 1] - a_cum_t[h:h + 1, :]
            w = cb * jnp.exp(jnp.where(tril, seg, -jnp.inf)) * dt_t[h:h + 1, :]
            y_diag = jnp.where(lane_head == r, _dot(cast(w), xg_m), y_diag)
        e_g = _expand_heads(e_acum[:, h_lo:h_lo + HEADS_PER_GROUP], HEADS_PER_GROUP)
        d_g = _expand_heads(decay[:, h_lo:h_lo + HEADS_PER_GROUP], HEADS_PER_GROUP)
        y_sc[:, g * GROUP_P:(g + 1) * GROUP_P] = (
            y_diag + y_off * e_g + dsk_ref[:, g * GROUP_P:(g + 1) * GROUP_P] * xg)
        xdec_t = cast(_dot_nt(eye_p, cast(xg * d_g)))
        states = _dot(xdec_t, bg)
        cdec = jnp.concatenate(
            [jnp.broadcast_to(e_last_t[h_lo + r:h_lo + r + 1, :], (SSM_HEAD_DIM, SSM_D_STATE))
             for r in range(HEADS_PER_GROUP)], axis=0)
        h_sc[g] = hg * cdec + states

    gated = y_sc[...] * (z_ref[...] * _sigmoid(z_ref[...]))
    ms = jnp.mean(gated * gated, axis=-1, keepdims=True)
    y_ref[...] = (gated * lax.rsqrt(ms + RMS_EPS) * ng_ref[...]).astype(y_ref.dtype)

    @pl.when(c == pl.num_programs(1) - 1)
    def _():
        hout_ref[0] = h_sc[...]


def ssd_mixer(proj, hist, h0, conv_w, conv_b, dt_bias, a_log, d_skip, norm_g, *, nb, seq, q, row0,
              mxu_dtype, out_dtype):
    assert seq % q == 0 and row0 % q == 0 and q >= CONV_HALO
    nc = seq // q
    blk0 = row0 // q
    pad_l = lambda v: jnp.pad(v.astype(f32), (0, LANES - v.shape[0])).reshape(1, LANES)
    a = -jnp.exp(a_log.astype(f32))
    row = lambda w, cb: pl.BlockSpec((q, w), lambda b, c: (blk0 + b * nc + c, cb))
    const = lambda r, w: pl.BlockSpec((r, w), lambda b, c: (0, 0))
    kern = functools.partial(_ssd_kernel, q=q, mxu_dtype=mxu_dtype)
    state_spec = pl.BlockSpec((1, SSM_GROUPS, GROUP_P, SSM_D_STATE), lambda b, c: (b, 0, 0, 0))
    return pl.pallas_call(
        kern,
        grid=(nb, nc),
        in_specs=[row(CONV_DIM, COL_XBC // CONV_DIM), row(SSM_D_INNER, COL_Z // SSM_D_INNER),
                  row(LANES, COL_DT // LANES),
                  pl.BlockSpec((1, CONV_HALO, CONV_DIM), lambda b, c: (b, 0, 0)), state_spec,
                  const(CONV_W, CONV_DIM), const(1, CONV_DIM), const(1, LANES), const(1, LANES),
                  const(1, SSM_D_INNER), const(1, SSM_D_INNER)],
        out_specs=[pl.BlockSpec((q, SSM_D_INNER), lambda b, c: (b * nc + c, 0)), state_spec],
        out_shape=[jax.ShapeDtypeStruct((nb * seq, SSM_D_INNER), out_dtype),
                   jax.ShapeDtypeStruct((nb, SSM_GROUPS, GROUP_P, SSM_D_STATE), f32)],
        scratch_shapes=[pltpu.VMEM((q + CONV_HALO, CONV_DIM), f32),
                        pltpu.VMEM((SSM_GROUPS, GROUP_P, SSM_D_STATE), f32),
                        pltpu.VMEM((q, SSM_D_INNER), f32)],
        compiler_params=_cparams(("parallel", "arbitrary"), 40),
        name="ssd_mixer",
    )(proj, proj, proj, hist, h0, conv_w.astype(f32), conv_b.reshape(1, CONV_DIM).astype(f32),
      pad_l(dt_bias), pad_l(a), jnp.repeat(d_skip.astype(f32), SSM_HEAD_DIM).reshape(1, SSM_D_INNER),
      norm_g.reshape(1, SSM_D_INNER).astype(f32))


def _pool_mix_kernel(cur_ref, halo_ref, o_ref, ext_sc, *, tm, pos0, zero_first_halo):
    i = pl.program_id(1)
    halo = halo_ref[0]
    if zero_first_halo:
        halo = jnp.where(i == 0, 0.0, halo)
    ext_sc[0:HALO, :] = halo
    ext_sc[HALO:HALO + tm, :] = cur_ref[0]
    gd = cur_ref.shape[2] // len(POOL_WINDOWS)
    count_pos = (pos0 + i * tm + 1 + lax.broadcasted_iota(jnp.int32, (tm, 1), 0)).astype(f32)
    for gi, win in enumerate(POOL_WINDOWS):
        cols = slice(gi * gd, (gi + 1) * gd)
        u = ext_sc[HALO:HALO + tm, cols]
        wsum = u
        for k in range(1, win):
            wsum = wsum + ext_sc[pl.ds(HALO - k, tm), cols]
        mean = wsum / jnp.minimum(count_pos, float(win))
        o_ref[0, :, cols] = (mean - u).astype(o_ref.dtype)


def pool_mix(u, halo, *, tm, pos0, zero_first_halo):
    nb, seq, d = u.shape
    nblk = tm // HALO
    kern = functools.partial(_pool_mix_kernel, tm=tm, pos0=pos0, zero_first_halo=zero_first_halo)
    return pl.pallas_call(
        kern,
        grid=(nb, seq // tm),
        in_specs=[pl.BlockSpec((1, tm, d), lambda b, i: (b, i, 0)),
                  pl.BlockSpec((1, HALO, d), lambda b, i: (b, jnp.maximum(i * nblk - 1, 0), 0))],
        out_specs=pl.BlockSpec((1, tm, d), lambda b, i: (b, i, 0)),
        out_shape=jax.ShapeDtypeStruct((nb, seq, d), bf16),
        scratch_shapes=[pltpu.VMEM((HALO + tm, d), f32)],
        compiler_params=_cparams(("parallel", "arbitrary"), 40),
        name="pool_mix",
    )(u, halo)


def _rope_tables(pos):
    inv_freq = 1.0 / (ROPE_THETA ** (jnp.arange(0, QK_ROPE, 2, dtype=f32) / QK_ROPE))
    ang = pos.astype(f32)[:, None] * inv_freq[None, :]
    cos, sin = jnp.cos(ang), jnp.sin(ang)
    zeros = jnp.zeros((pos.shape[0], ROPE_PAD - QK_ROPE), f32)
    return (jnp.concatenate([cos, cos, zeros], axis=1), jnp.concatenate([-sin, sin, zeros], axis=1))


def _swap_halves(w):
    half = w.shape[-1] // 2
    return jnp.concatenate([w[..., half:], w[..., :half]], axis=-1)


def _pad_last(w, n):
    return jnp.pad(w, [(0, 0)] * (w.ndim - 1) + [(0, n - w.shape[-1])])


def kernel(x_prompt, x_sample, cache_mla_latent, cache_mla_rope, state_ssm_conv, state_ssm, state_pool, page_table, norm_mix_g, norm_ffn_g, norm_final_g, w_in, q_norm_g, kv_norm_g, w_uq, w_ukv, conv_w, conv_b, dt_bias, a_log, d_skip, ssm_norm_g, w_out, w_pool, pool_scale, w_gate, w_up, w_down):
    bp, lp, d_model = x_prompt.shape
    bs, ls, _ = x_sample.shape
    mp, ms_rows = bp * lp, bs * ls
    m = mp + ms_rows
    li = 0

    wi = w_in[li]
    s_q, s_kv, s_kr, s_z, s_xbc = Q_LORA, Q_LORA + KV_LORA, Q_LORA + KV_LORA + QK_ROPE, 0, 0
    o_z = s_kr
    o_xbc = o_z + SSM_D_INNER
    o_dt = o_xbc + CONV_DIM
    w_kr = wi[:, s_kv:s_kr]
    w_in_r = jnp.concatenate(
        [wi[:, o_xbc:o_dt], wi[:, o_z:o_xbc], wi[:, :s_q], wi[:, s_q:s_kv],
         _pad_last(w_kr, ROPE_PAD), _pad_last(_swap_halves(w_kr), ROPE_PAD),
         _pad_last(wi[:, o_dt:], 2 * LANES)], axis=1).astype(bf16)
    assert w_in_r.shape[1] == PROJ_W
    wq = w_uq[li]
    wq_nope = wq[:, :, :QK_NOPE].transpose(1, 0, 2).astype(bf16)
    wq_rope = wq[:, :, QK_NOPE:].transpose(1, 0, 2)
    wq_r = _pad_last(wq_rope, ROPE_PAD).astype(bf16)
    wq_rr = _pad_last(_swap_halves(wq_rope), ROPE_PAD).astype(bf16)
    wkv = w_ukv[li]
    w_uk = wkv[:, :, :QK_NOPE].transpose(1, 2, 0).astype(bf16)
    w_uv = wkv[:, :, QK_NOPE:].transpose(1, 0, 2).astype(bf16)
    w_out_b = w_out[li].astype(bf16)
    w_pool_b = w_pool[0].astype(bf16)
    w_gate_b = w_gate.astype(bf16)
    w_up_b = w_up.astype(bf16)
    w_down_b = w_down.astype(bf16)

    pos = jnp.concatenate([jnp.tile(jnp.arange(lp), bp), jnp.tile(PAST_LEN + jnp.arange(ls), bs)])
    cos_t, sin_t = _rope_tables(pos)

    x0 = jnp.concatenate([x_prompt.reshape(mp, d_model), x_sample.reshape(ms_rows, d_model)], axis=0)

    u = rmsnorm(x0, norm_mix_g[0], tm=256, out_dtype=bf16)
    proj = matmul(u, w_in_r, tm=1024, tn=256, out_dtype=f32, name="in_proj")
    ckv, ckv_b, kpe, kpe_b = kv_prep(proj, cos_t, sin_t, kv_norm_g[li], tm=512)

    qlat_p, qpe_p = q_prep(proj, cos_t, sin_t, q_norm_g[li], wq_nope, wq_r, wq_rr, w_uk,
                           rows=mp, row_blk0=0, tm=512, out_dtype=bf16)
    qlat_s, qpe_s = q_prep(proj, cos_t, sin_t, q_norm_g[li], wq_nope, wq_r, wq_rr, w_uk,
                           rows=ms_rows, row_blk0=mp // 512, tm=512, out_dtype=f32)
    o_p = attn_prompt(qlat_p, qpe_p, ckv_b[:mp].reshape(bp, lp, KV_LORA), kpe_b[:mp].reshape(bp, lp, ROPE_PAD),
                      w_uv, nb=bp, seq=lp, tq=128, tk=256)
    o_s = attn_sample(page_table, qlat_s, qpe_s, ckv, kpe, cache_mla_latent, cache_mla_rope, w_uv,
                      li=li, row0=mp, t_new=ls, pages_per_step=8)

    hist_p = jnp.zeros((bp, CONV_HALO, CONV_DIM), f32)
    hist_s = jnp.pad(state_ssm_conv[li], ((0, 0), (CONV_HALO - (CONV_W - 1), 0), (0, 0)))
    h0_p = jnp.zeros((bp, SSM_GROUPS, GROUP_P, SSM_D_STATE), f32)
    h0_s = state_ssm[li].reshape(bs, SSM_GROUPS, GROUP_P, SSM_D_STATE)
    ssd_w = (conv_w[li], conv_b[li], dt_bias[li], a_log[li], d_skip[li], ssm_norm_g[li])
    y_p, h_p = ssd_mixer(proj, hist_p, h0_p, *ssd_w, nb=bp, seq=lp, q=SSD_CHUNK, row0=0,
                         mxu_dtype=bf16, out_dtype=bf16)
    y_s, h_s = ssd_mixer(proj, hist_s, h0_s, *ssd_w, nb=bs, seq=ls, q=ls, row0=mp,
                         mxu_dtype=f32, out_dtype=f32)

    mix = jnp.concatenate(
        [jnp.concatenate([o_p.reshape(mp, -1), y_p], axis=1),
         jnp.concatenate([o_s.reshape(ms_rows, -1), y_s], axis=1).astype(bf16)], axis=0)
    x1 = matmul(mix, w_out_b, tm=1024, tn=256, out_dtype=f32, residual=x0, name="out_proj")

    def ffn(x, layer):
        uf = rmsnorm(x, norm_ffn_g[layer], tm=256, out_dtype=bf16)
        hid = ffn_up(uf, w_gate_b[layer], w_up_b[layer], tm=1024, tn=256)
        return matmul(hid, w_down_b[layer], tm=512, tn=256, out_dtype=f32, residual=x, name="ffn_down")

    x2 = ffn(x1, 0)

    u1 = rmsnorm(x2, norm_mix_g[1], tm=256, out_dtype=f32)
    u1_p = u1[:mp].reshape(bp, lp, d_model)
    u1_s = u1[mp:].reshape(bs, ls, d_model)
    halo_s = jnp.pad(state_pool[0], ((0, 0), (HALO - POOL_HIST, 0), (0, 0)))
    mix_p = pool_mix(u1_p, u1_p, tm=256, pos0=0, zero_first_halo=True)
    mix_s = pool_mix(u1_s, halo_s, tm=ls, pos0=PAST_LEN, zero_first_halo=False)
    mix1 = jnp.concatenate([mix_p.reshape(mp, d_model), mix_s.reshape(ms_rows, d_model)], axis=0)
    x3 = pool_matmul(mix1, w_pool_b, pool_scale[0], x2, tm=1024, tn=512)
    x4 = ffn(x3, 1)

    y_prompt = rmsnorm(x4, norm_final_g, tm=256, out_dtype=f32, rows=mp).reshape(bp, lp, d_model)
    y_sample = rmsnorm(x4, norm_final_g, tm=256, out_dtype=f32, rows=ms_rows,
                       row_blk0=mp // 256).reshape(bs, ls, d_model)

    new_pool_p = u1_p[:, lp - POOL_HIST:]
    new_pool_s = jnp.concatenate([state_pool[0], u1_s], axis=1)[:, -POOL_HIST:]
    conv_raw = proj[:, COL_XBC:COL_XBC + CONV_DIM]
    new_conv_p = conv_raw[:mp].reshape(bp, lp, CONV_DIM)[:, lp - (CONV_W - 1):]
    new_conv_s = jnp.concatenate([state_ssm_conv[li], conv_raw[mp:].reshape(bs, ls, CONV_DIM)],
                                 axis=1)[:, -(CONV_W - 1):]
    st = lambda h, nb: h.reshape(nb, SSM_HEADS, SSM_HEAD_DIM, SSM_D_STATE)
    return (y_prompt, y_sample,
            ckv[:mp].reshape(1, bp, lp, KV_LORA), kpe[:mp, :QK_ROPE].reshape(1, bp, lp, QK_ROPE),
            ckv[mp:].reshape(1, bs, ls, KV_LORA), kpe[mp:, :QK_ROPE].reshape(1, bs, ls, QK_ROPE),
            new_conv_p[None], new_conv_s[None], st(h_p, bp)[None], st(h_s, bs)[None],
            new_pool_p[None], new_pool_s[None])
```

```python
import functools

import jax
import jax.numpy as jnp
from jax import lax
from jax.experimental import pallas as pl
from jax.experimental.pallas import tpu as pltpu

f32 = jnp.float32
bf16 = jnp.bfloat16

RMS_EPS = 1e-6
ROPE_THETA = 10000.0
PAST_LEN = 16384
PAGE_SIZE = 128
MLA_HEADS = 16
QK_NOPE = 128
QK_ROPE = 64
V_DIM = 128
Q_LORA = 1024
KV_LORA = 512
ATTN_SCALE = (QK_NOPE + QK_ROPE) ** -0.5
SSM_D_INNER = 2048
SSM_HEAD_DIM = 64
SSM_HEADS = 32
SSM_GROUPS = 8
SSM_D_STATE = 128
CONV_W = 4
CONV_DIM = SSM_D_INNER + 2 * SSM_GROUPS * SSM_D_STATE
SSD_CHUNK = 128
POOL_WINDOWS = (2, 4, 8, 16)
POOL_HIST = 15

LANES = 128
SUBLANES = 8
GROUP_P = SSM_D_INNER // SSM_GROUPS
HEADS_PER_GROUP = SSM_HEADS // SSM_GROUPS
ROPE_PAD = LANES
QK_CAT = KV_LORA + ROPE_PAD
HALO = 16
CONV_HALO = SUBLANES

COL_XBC = 0
COL_Z = CONV_DIM
COL_Q = COL_Z + SSM_D_INNER
COL_KV = COL_Q + Q_LORA
COL_KR = COL_KV + KV_LORA
COL_KRR = COL_KR + ROPE_PAD
COL_DT = COL_KRR + ROPE_PAD
PROJ_W = COL_DT + 2 * LANES


def _dot(a, b):
    return jnp.dot(a, b, preferred_element_type=f32)


def _dot_nt(a, b):
    return lax.dot_general(a, b, (((1,), (1,)), ((), ())), preferred_element_type=f32)


def _sigmoid(x):
    return 1.0 / (1.0 + jnp.exp(-x))


def _split3(x):
    hi = x.astype(bf16).astype(f32)
    r1 = x - hi
    mid = r1.astype(bf16).astype(f32)
    lo = (r1 - mid).astype(bf16).astype(f32)
    return hi, mid, lo


def _cparams(sem, vmem_mib):
    return pltpu.CompilerParams(dimension_semantics=sem, vmem_limit_bytes=vmem_mib << 20)


def _norm_kernel(x_ref, g_ref, o_ref):
    x = x_ref[...].astype(f32)
    ms = jnp.mean(x * x, axis=-1, keepdims=True)
    o_ref[...] = (x * lax.rsqrt(ms + RMS_EPS) * g_ref[...]).astype(o_ref.dtype)


def rmsnorm(x, g, *, tm, out_dtype, rows=None, row_blk0=0, col_blk=0, width=None):
    rows = x.shape[0] if rows is None else rows
    width = x.shape[1] if width is None else width
    return pl.pallas_call(
        _norm_kernel,
        grid=(rows // tm,),
        in_specs=[pl.BlockSpec((tm, width), lambda i: (i + row_blk0, col_blk)),
                  pl.BlockSpec((1, width), lambda i: (0, 0))],
        out_specs=pl.BlockSpec((tm, width), lambda i: (i, 0)),
        out_shape=jax.ShapeDtypeStruct((rows, width), out_dtype),
        compiler_params=_cparams(("parallel",), 40),
        name="rmsnorm",
    )(x, g.reshape(1, width).astype(f32))


def _mm_kernel(a_ref, w_ref, o_ref):
    o_ref[...] = _dot(a_ref[...], w_ref[...]).astype(o_ref.dtype)


def _mm_res_kernel(a_ref, w_ref, r_ref, o_ref):
    o_ref[...] = (_dot(a_ref[...], w_ref[...]) + r_ref[...]).astype(o_ref.dtype)


def matmul(a, w, *, tm, tn, out_dtype, residual=None, name="matmul"):
    m, k = a.shape
    n = w.shape[1]
    in_specs = [pl.BlockSpec((tm, k), lambda i, j: (i, 0)),
                pl.BlockSpec((k, tn), lambda i, j: (0, j))]
    args = [a, w]
    body = _mm_kernel
    if residual is not None:
        in_specs.append(pl.BlockSpec((tm, tn), lambda i, j: (i, j)))
        args.append(residual)
        body = _mm_res_kernel
    return pl.pallas_call(
        body,
        grid=(m // tm, n // tn),
        in_specs=in_specs,
        out_specs=pl.BlockSpec((tm, tn), lambda i, j: (i, j)),
        out_shape=jax.ShapeDtypeStruct((m, n), out_dtype),
        compiler_params=_cparams(("parallel", "arbitrary"), 56),
        name=name,
    )(*args)


def _ffn_up_kernel(a_ref, wg_ref, wu_ref, o_ref):
    a = a_ref[...]
    g = _dot(a, wg_ref[...])
    u = _dot(a, wu_ref[...])
    o_ref[...] = (g * _sigmoid(g) * u).astype(o_ref.dtype)


def ffn_up(a, wg, wu, *, tm, tn):
    m, k = a.shape
    n = wg.shape[1]
    return pl.pallas_call(
        _ffn_up_kernel,
        grid=(m // tm, n // tn),
        in_specs=[pl.BlockSpec((tm, k), lambda i, j: (i, 0)),
                  pl.BlockSpec((k, tn), lambda i, j: (0, j)),
                  pl.BlockSpec((k, tn), lambda i, j: (0, j))],
        out_specs=pl.BlockSpec((tm, tn), lambda i, j: (i, j)),
        out_shape=jax.ShapeDtypeStruct((m, n), bf16),
        compiler_params=_cparams(("parallel", "arbitrary"), 56),
        name="ffn_up",
    )(a, wg, wu)


def _pool_mm_kernel(a_ref, w_ref, s_ref, r_ref, o_ref):
    o_ref[...] = _dot(a_ref[...], w_ref[0]) * s_ref[...] + r_ref[...]


def pool_matmul(mix, w, scale, residual, *, tm, tn):
    m, d = mix.shape
    ng, gd, _ = w.shape
    nj = gd // tn
    return pl.pallas_call(
        _pool_mm_kernel,
        grid=(m // tm, ng, nj),
        in_specs=[pl.BlockSpec((tm, gd), lambda i, g, j: (i, g)),
                  pl.BlockSpec((1, gd, tn), lambda i, g, j: (g, 0, j)),
                  pl.BlockSpec((1, tn), lambda i, g, j: (0, g * nj + j)),
                  pl.BlockSpec((tm, tn), lambda i, g, j: (i, g * nj + j))],
        out_specs=pl.BlockSpec((tm, tn), lambda i, g, j: (i, g * nj + j)),
        out_shape=jax.ShapeDtypeStruct((m, d), f32),
        compiler_params=_cparams(("parallel", "arbitrary", "arbitrary"), 40),
        name="pool_matmul",
    )(mix, w, scale.reshape(1, d).astype(f32), residual)


def _kv_prep_kernel(kv_ref, kr_ref, krr_ref, cos_ref, sin_ref, g_ref, ckv_ref, kpe_ref, kcat_ref):
    x = kv_ref[...]
    ms = jnp.mean(x * x, axis=-1, keepdims=True)
    ckv = x * lax.rsqrt(ms + RMS_EPS) * g_ref[...]
    kpe = kr_ref[...] * cos_ref[...] + krr_ref[...] * sin_ref[...]
    ckv_ref[...] = ckv
    kpe_ref[...] = kpe
    kcat_ref[:, :KV_LORA] = ckv.astype(bf16)
    kcat_ref[:, KV_LORA:] = kpe.astype(bf16)


def kv_prep(proj, cos_t, sin_t, kv_norm_g, *, tm):
    m = proj.shape[0]
    row = lambda w, cb: pl.BlockSpec((tm, w), lambda i: (i, cb))
    return pl.pallas_call(
        _kv_prep_kernel,
        grid=(m // tm,),
        in_specs=[row(KV_LORA, COL_KV // KV_LORA), row(ROPE_PAD, COL_KR // ROPE_PAD),
                  row(ROPE_PAD, COL_KRR // ROPE_PAD), row(ROPE_PAD, 0), row(ROPE_PAD, 0),
                  pl.BlockSpec((1, KV_LORA), lambda i: (0, 0))],
        out_specs=[row(KV_LORA, 0), row(ROPE_PAD, 0), row(QK_CAT, 0)],
        out_shape=[jax.ShapeDtypeStruct((m, KV_LORA), f32), jax.ShapeDtypeStruct((m, ROPE_PAD), f32),
                   jax.ShapeDtypeStruct((m, QK_CAT), bf16)],
        compiler_params=_cparams(("parallel",), 32),
        name="kv_prep",
    )(proj, proj, proj, cos_t, sin_t, kv_norm_g.reshape(1, KV_LORA).astype(f32))


def _q_prep_kernel(cq_ref, g_ref, wn_ref, wr_ref, wrr_ref, wuk_ref, cos_ref, sin_ref, q_ref, cqn_sc):
    @pl.when(pl.program_id(1) == 0)
    def _():
        x = cq_ref[...]
        ms = jnp.mean(x * x, axis=-1, keepdims=True)
        cqn_sc[...] = (x * lax.rsqrt(ms + RMS_EPS) * g_ref[...]).astype(bf16)

    cqn = cqn_sc[...]
    q_nope = _dot(cqn, wn_ref[0]).astype(bf16)
    q_lat = _dot(q_nope, wuk_ref[0])
    q_r = _dot(cqn, wr_ref[0])
    q_rr = _dot(cqn, wrr_ref[0])
    q_pe = q_r * cos_ref[...] + q_rr * sin_ref[...]
    q_ref[0, :, :KV_LORA] = (q_lat * ATTN_SCALE).astype(q_ref.dtype)
    q_ref[0, :, KV_LORA:] = (q_pe * ATTN_SCALE).astype(q_ref.dtype)


def q_prep(proj, cos_t, sin_t, q_norm_g, wn, wr, wrr, wuk, *, rows, row_blk0, tm, out_dtype):
    nh = wn.shape[0]
    hw = lambda a, b: pl.BlockSpec((1, a, b), lambda i, h: (h, 0, 0))
    return pl.pallas_call(
        _q_prep_kernel,
        grid=(rows // tm, nh),
        in_specs=[pl.BlockSpec((tm, Q_LORA), lambda i, h: (i + row_blk0, COL_Q // Q_LORA)),
                  pl.BlockSpec((1, Q_LORA), lambda i, h: (0, 0)),
                  hw(Q_LORA, QK_NOPE), hw(Q_LORA, ROPE_PAD), hw(Q_LORA, ROPE_PAD), hw(QK_NOPE, KV_LORA),
                  pl.BlockSpec((tm, ROPE_PAD), lambda i, h: (i + row_blk0, 0)),
                  pl.BlockSpec((tm, ROPE_PAD), lambda i, h: (i + row_blk0, 0))],
        out_specs=pl.BlockSpec((1, tm, QK_CAT), lambda i, h: (h, i, 0)),
        out_shape=jax.ShapeDtypeStruct((nh, rows, QK_CAT), out_dtype),
        scratch_shapes=[pltpu.VMEM((tm, Q_LORA), bf16)],
        compiler_params=_cparams(("parallel", "arbitrary"), 32),
        name="q_prep",
    )(proj, q_norm_g.reshape(1, Q_LORA).astype(f32), wn, wr, wrr, wuk, cos_t, sin_t)


def _attn_prompt_kernel(q_ref, k_ref, wuv_ref, o_ref, m_sc, l_sc, acc_sc, *, tq, tk, row_chunks):
    nh = q_ref.shape[0]
    hpc = nh // row_chunks
    rc = hpc * tq
    qb = pl.program_id(1)
    m_sc[...] = jnp.full(m_sc.shape, -1e30, f32)
    l_sc[...] = jnp.zeros(l_sc.shape, f32)
    acc_sc[...] = jnp.zeros(acc_sc.shape, f32)

    def step(kb, masked):
        k = k_ref[0, pl.ds(pl.multiple_of(kb * tk, tk), tk), :]
        v = k[:, :KV_LORA]
        for r in range(row_chunks):
            rs = slice(r * rc, (r + 1) * rc)
            q = q_ref[r * hpc:(r + 1) * hpc].reshape(rc, QK_CAT)
            s = _dot_nt(q, k)
            if masked:
                qpos = qb * tq + lax.broadcasted_iota(jnp.int32, (hpc, tq, tk), 1).reshape(rc, tk)
                kpos = kb * tk + lax.broadcasted_iota(jnp.int32, (rc, tk), 1)
                s = jnp.where(kpos <= qpos, s, -jnp.inf)
            m_prev = m_sc[rs]
            m_new = jnp.maximum(m_prev, jnp.max(s, axis=-1, keepdims=True))
            alpha = jnp.exp(m_prev - m_new)
            p = jnp.exp(s - m_new)
            l_sc[rs] = alpha * l_sc[rs] + jnp.sum(p, axis=-1, keepdims=True)
            acc_sc[rs] = alpha * acc_sc[rs] + _dot(p.astype(bf16), v)
            m_sc[rs] = m_new

    n_full = (qb * tq) // tk

    def body(kb, carry):
        step(kb, False)
        return carry

    lax.fori_loop(0, n_full, body, 0)
    step(n_full, True)

    o = (acc_sc[...] / l_sc[...]).astype(bf16).reshape(nh, tq, KV_LORA)
    for h in range(nh):
        o_ref[0, :, h * V_DIM:(h + 1) * V_DIM] = _dot(o[h], wuv_ref[h]).astype(o_ref.dtype)


def attn_prompt(q, kcat, wuv, *, nb, seq, tq, tk, row_chunks):
    nh = q.shape[0]
    assert tk % tq == 0 and seq % tk == 0 and nh % row_chunks == 0
    nqb = seq // tq
    rows = nh * tq
    kern = functools.partial(_attn_prompt_kernel, tq=tq, tk=tk, row_chunks=row_chunks)
    return pl.pallas_call(
        kern,
        grid=(nb, nqb),
        in_specs=[pl.BlockSpec((nh, tq, QK_CAT), lambda b, i: (0, b * nqb + i, 0)),
                  pl.BlockSpec((1, seq, QK_CAT), lambda b, i: (b, 0, 0)),
                  pl.BlockSpec((nh, KV_LORA, V_DIM), lambda b, i: (0, 0, 0))],
        out_specs=pl.BlockSpec((1, tq, nh * V_DIM), lambda b, i: (b, i, 0)),
        out_shape=jax.ShapeDtypeStruct((nb, seq, nh * V_DIM), bf16),
        scratch_shapes=[pltpu.VMEM((rows, 1), f32), pltpu.VMEM((rows, 1), f32),
                        pltpu.VMEM((rows, KV_LORA), f32)],
        compiler_params=_cparams(("parallel", "arbitrary"), 48),
        name="attn_prompt",
    )(q, kcat, wuv)


def _attn_sample_kernel(pt_ref, q_ref, slat_ref, spe_ref, wuv_ref, clat_hbm, crope_hbm, o_ref,
                        lat_buf, rope_buf, sem, m_sc, l_sc, acc_sc, q_sc, *, li, ch, n_sub, t_new):
    nh = q_ref.shape[0]
    rows = nh * t_new
    n_chunks = pt_ref.shape[1] // ch
    ps = ch // n_sub
    b = pl.program_id(0)
    nb = pl.num_programs(0)

    def page_copies(pid, slot, j):
        return (pltpu.make_async_copy(clat_hbm.at[li, pid], lat_buf.at[slot, j], sem.at[0, slot]),
                pltpu.make_async_copy(crope_hbm.at[li, pid], rope_buf.at[slot, j], sem.at[1, slot]))

    def start_chunk(bb, c, slot):
        for j in range(ch):
            for cp in page_copies(pt_ref[bb, c * ch + j], slot, j):
                cp.start()

    def wait_chunk(slot):
        for j in range(ch):
            for cp in page_copies(0, slot, j):
                cp.wait()

    @pl.when(b == 0)
    def _():
        start_chunk(0, 0, 0)

    q = q_ref[...].reshape(rows, QK_CAT).astype(bf16)
    q_sc[...] = q
    pad = 2 * SUBLANES - t_new
    ks = jnp.concatenate([jnp.concatenate([slat_ref[...], spe_ref[...]], axis=1),
                          jnp.zeros((pad, QK_CAT), f32)], axis=0).astype(bf16)
    s = _dot_nt(q, ks)
    tpos = lax.broadcasted_iota(jnp.int32, (nh, t_new, 2 * SUBLANES), 1).reshape(rows, 2 * SUBLANES)
    kpos = lax.broadcasted_iota(jnp.int32, (rows, 2 * SUBLANES), 1)
    s = jnp.where(kpos <= tpos, s, -jnp.inf)
    m0 = jnp.maximum(jnp.max(s, axis=-1, keepdims=True), -1e30)
    p = jnp.exp(s - m0)
    m_sc[...] = m0
    l_sc[...] = jnp.sum(p, axis=-1, keepdims=True)
    acc_sc[...] = _dot(p.astype(bf16), ks[:, :KV_LORA])

    def softmax_part(slot, sub):
        k = lat_buf[slot, sub * ps:(sub + 1) * ps].reshape(ps * PAGE_SIZE, KV_LORA).astype(bf16)
        kr_t = jnp.concatenate([rope_buf[slot, sub * ps + j] for j in range(ps)], axis=1).astype(bf16)
        s = _dot_nt(q_sc[:, :KV_LORA], k) + _dot(q_sc[:, KV_LORA:KV_LORA + QK_ROPE], kr_t)
        m_s = jnp.max(s, axis=-1, keepdims=True)
        p = jnp.exp(s - m_s)
        return m_s, jnp.sum(p, axis=-1, keepdims=True), _dot(p.astype(bf16), k)

    def merge(parts):
        m_old = m_sc[...]
        m_new = m_old
        for m_s, _, _ in parts:
            m_new = jnp.maximum(m_new, m_s)
        alpha = jnp.exp(m_old - m_new)
        l = alpha * l_sc[...]
        acc = alpha * acc_sc[...]
        for m_s, l_s, o_s in parts:
            w = jnp.exp(m_s - m_new)
            l = l + w * l_s
            acc = acc + w * o_s
        m_sc[...] = m_new
        l_sc[...] = l
        acc_sc[...] = acc

    def chunk(c, carry):
        slot = c & 1
        wait_chunk(slot)
        wrap = c + 1 == n_chunks
        bb = jnp.where(wrap, b + 1, b)
        cc = jnp.where(wrap, 0, c + 1)

        @pl.when(bb < nb)
        def _():
            start_chunk(bb, cc, 1 - slot)

        merge([softmax_part(slot, sub) for sub in range(n_sub)])
        return carry

    lax.fori_loop(0, n_chunks, chunk, 0)

    o = (acc_sc[...] / l_sc[...]).astype(bf16).astype(f32).reshape(nh, t_new, KV_LORA)
    for h in range(nh):
        o_ref[0, :, h * V_DIM:(h + 1) * V_DIM] = _dot(o[h], wuv_ref[h].astype(f32))


def attn_sample(page_table, q, ckv, kpe, cache_lat, cache_rope_t, wuv, *, li, row0, t_new, ch, n_sub):
    nh = q.shape[0]
    nb, n_pages = page_table.shape
    assert n_pages % (2 * ch) == 0 and ch % n_sub == 0 and row0 % t_new == 0
    rows = nh * t_new
    blk0 = row0 // t_new
    kern = functools.partial(_attn_sample_kernel, li=li, ch=ch, n_sub=n_sub, t_new=t_new)
    grid_spec = pltpu.PrefetchScalarGridSpec(
        num_scalar_prefetch=1,
        grid=(nb,),
        in_specs=[pl.BlockSpec((nh, t_new, QK_CAT), lambda b, pt: (0, b, 0)),
                  pl.BlockSpec((t_new, KV_LORA), lambda b, pt: (blk0 + b, 0)),
                  pl.BlockSpec((t_new, ROPE_PAD), lambda b, pt: (blk0 + b, 0)),
                  pl.BlockSpec((nh, KV_LORA, V_DIM), lambda b, pt: (0, 0, 0)),
                  pl.BlockSpec(memory_space=pl.ANY),
                  pl.BlockSpec(memory_space=pl.ANY)],
        out_specs=pl.BlockSpec((1, t_new, nh * V_DIM), lambda b, pt: (b, 0, 0)),
        scratch_shapes=[pltpu.VMEM((2, ch, PAGE_SIZE, KV_LORA), f32),
                        pltpu.VMEM((2, ch, QK_ROPE, PAGE_SIZE), f32),
                        pltpu.SemaphoreType.DMA((2, 2)),
                        pltpu.VMEM((rows, 1), f32), pltpu.VMEM((rows, 1), f32),
                        pltpu.VMEM((rows, KV_LORA), f32),
                        pltpu.VMEM((rows, QK_CAT), bf16)],
    )
    return pl.pallas_call(
        kern,
        grid_spec=grid_spec,
        out_shape=jax.ShapeDtypeStruct((nb, t_new, nh * V_DIM), f32),
        compiler_params=_cparams(("arbitrary",), 32),
        name="attn_sample",
    )(page_table, q, ckv, kpe, wuv, cache_lat, cache_rope_t)


def _expand_heads(cols, n):
    q = cols.shape[0]
    lane_head = lax.broadcasted_iota(jnp.int32, (q, GROUP_P), 1) // SSM_HEAD_DIM
    out = jnp.broadcast_to(cols[:, 0:1], (q, GROUP_P))
    for r in range(1, n):
        out = jnp.where(lane_head == r, cols[:, r:r + 1], out)
    return out


def _ssd_kernel(xbc_ref, z_ref, dt_ref, hist_ref, h0_ref, cw_ref, cb_ref, dtb_ref, a_ref, dsk_ref, ng_ref,
                y_ref, hout_ref, ext_sc, h_sc, y_sc, *, q, mxu_dtype):
    c = pl.program_id(1)
    cast = lambda v: v.astype(mxu_dtype)

    @pl.when(c == 0)
    def _():
        ext_sc[0:CONV_HALO, :] = hist_ref[0]
        h_sc[...] = h0_ref[0]

    ext_sc[CONV_HALO:CONV_HALO + q, :] = xbc_ref[...]
    conv = cb_ref[...]
    for k in range(CONV_W):
        conv = conv + ext_sc[pl.ds(CONV_HALO - (CONV_W - 1) + k, q), :] * cw_ref[k:k + 1, :]
    ext_sc[0:CONV_HALO, :] = ext_sc[q:q + CONV_HALO, :]
    act = conv * _sigmoid(conv)

    dt_in = dt_ref[...] + dtb_ref[...]
    dt = jnp.maximum(dt_in, 0.0) + jnp.log1p(jnp.exp(-jnp.abs(dt_in)))
    dta = dt * a_ref[...]
    ri = lax.broadcasted_iota(jnp.int32, (q, q), 0)
    ci = lax.broadcasted_iota(jnp.int32, (q, q), 1)
    tril = ci <= ri
    trif = tril.astype(f32)
    eye_l = (lax.broadcasted_iota(jnp.int32, (LANES, LANES), 0)
             == lax.broadcasted_iota(jnp.int32, (LANES, LANES), 1)).astype(f32)
    eye_p = (lax.broadcasted_iota(jnp.int32, (GROUP_P, GROUP_P), 0)
             == lax.broadcasted_iota(jnp.int32, (GROUP_P, GROUP_P), 1)).astype(mxu_dtype)
    a_cum = sum(_dot(trif, part) for part in _split3(dta))
    a_cum_t = sum(_dot_nt(eye_l, part) for part in _split3(a_cum))
    dt_t = sum(_dot_nt(eye_l, part) for part in _split3(dt))
    a_last = a_cum[q - 1:q, :]
    decay = jnp.exp(a_last - a_cum) * dt
    e_acum = jnp.exp(a_cum)
    e_last_t = jnp.exp(a_cum_t[:, q - 1:q])
    lane_head = lax.broadcasted_iota(jnp.int32, (q, GROUP_P), 1) // SSM_HEAD_DIM

    for g in range(SSM_GROUPS):
        h_lo = g * HEADS_PER_GROUP
        bg = cast(act[:, SSM_D_INNER + g * SSM_D_STATE:SSM_D_INNER + (g + 1) * SSM_D_STATE])
        cg = cast(act[:, SSM_D_INNER + (SSM_GROUPS + g) * SSM_D_STATE:
                      SSM_D_INNER + (SSM_GROUPS + g + 1) * SSM_D_STATE])
        xg = act[:, g * GROUP_P:(g + 1) * GROUP_P]
        xg_m = cast(xg)
        cb = _dot_nt(cg, bg)
        hg = h_sc[g]
        y_off = _dot_nt(cg, cast(hg))
        y_diag = jnp.zeros((q, GROUP_P), f32)
        for r in range(HEADS_PER_GROUP):
            h = h_lo + r
            seg = a_cum[:, h:h + 1] - a_cum_t[h:h + 1, :]
            w = cb * jnp.exp(jnp.where(tril, seg, -jnp.inf)) * dt_t[h:h + 1, :]
            y_diag = jnp.where(lane_head == r, _dot(cast(w), xg_m), y_diag)
        e_g = _expand_heads(e_acum[:, h_lo:h_lo + HEADS_PER_GROUP], HEADS_PER_GROUP)
        d_g = _expand_heads(decay[:, h_lo:h_lo + HEADS_PER_GROUP], HEADS_PER_GROUP)
        y_sc[:, g * GROUP_P:(g + 1) * GROUP_P] = (
            y_diag + y_off * e_g + dsk_ref[:, g * GROUP_P:(g + 1) * GROUP_P] * xg)
        xdec_t = cast(_dot_nt(eye_p, cast(xg * d_g)))
        states = _dot(xdec_t, bg)
        cdec = jnp.concatenate(
            [jnp.broadcast_to(e_last_t[h_lo + r:h_lo + r + 1, :], (SSM_HEAD_DIM, SSM_D_STATE))
             for r in range(HEADS_PER_GROUP)], axis=0)
        h_sc[g] = hg * cdec + states

    gated = y_sc[...] * (z_ref[...] * _sigmoid(z_ref[...]))
    ms = jnp.mean(gated * gated, axis=-1, keepdims=True)
    y_ref[...] = (gated * lax.rsqrt(ms + RMS_EPS) * ng_ref[...]).astype(y_ref.dtype)

    @pl.when(c == pl.num_programs(1) - 1)
    def _():
        hout_ref[0] = h_sc[...]


def ssd_mixer(proj, hist, h0, conv_w, conv_b, dt_bias, a_log, d_skip, norm_g, *, nb, seq, q, row0,
              mxu_dtype, out_dtype):
    assert seq % q == 0 and row0 % q == 0 and q >= CONV_HALO
    nc = seq // q
    blk0 = row0 // q
    pad_l = lambda v: jnp.pad(v.astype(f32), (0, LANES - v.shape[0])).reshape(1, LANES)
    a = -jnp.exp(a_log.astype(f32))
    row = lambda w, cb: pl.BlockSpec((q, w), lambda b, c: (blk0 + b * nc + c, cb))
    const = lambda r, w: pl.BlockSpec((r, w), lambda b, c: (0, 0))
    kern = functools.partial(_ssd_kernel, q=q, mxu_dtype=mxu_dtype)
    state_spec = pl.BlockSpec((1, SSM_GROUPS, GROUP_P, SSM_D_STATE), lambda b, c: (b, 0, 0, 0))
    return pl.pallas_call(
        kern,
        grid=(nb, nc),
        in_specs=[row(CONV_DIM, COL_XBC // CONV_DIM), row(SSM_D_INNER, COL_Z // SSM_D_INNER),
                  row(LANES, COL_DT // LANES),
                  pl.BlockSpec((1, CONV_HALO, CONV_DIM), lambda b, c: (b, 0, 0)), state_spec,
                  const(CONV_W, CONV_DIM), const(1, CONV_DIM), const(1, LANES), const(1, LANES),
                  const(1, SSM_D_INNER), const(1, SSM_D_INNER)],
        out_specs=[pl.BlockSpec((q, SSM_D_INNER), lambda b, c: (b * nc + c, 0)), state_spec],
        out_shape=[jax.ShapeDtypeStruct((nb * seq, SSM_D_INNER), out_dtype),
                   jax.ShapeDtypeStruct((nb, SSM_GROUPS, GROUP_P, SSM_D_STATE), f32)],
        scratch_shapes=[pltpu.VMEM((q + CONV_HALO, CONV_DIM), f32),
                        pltpu.VMEM((SSM_GROUPS, GROUP_P, SSM_D_STATE), f32),
                        pltpu.VMEM((q, SSM_D_INNER), f32)],
        compiler_params=_cparams(("parallel", "arbitrary"), 40),
        name="ssd_mixer",
    )(proj, proj, proj, hist, h0, conv_w.astype(f32), conv_b.reshape(1, CONV_DIM).astype(f32),
      pad_l(dt_bias), pad_l(a), jnp.repeat(d_skip.astype(f32), SSM_HEAD_DIM).reshape(1, SSM_D_INNER),
      norm_g.reshape(1, SSM_D_INNER).astype(f32))


def _pool_mix_kernel(cur_ref, halo_ref, o_ref, ext_sc, *, tm, pos0, zero_first_halo):
    i = pl.program_id(1)
    halo = halo_ref[...]
    if zero_first_halo:
        halo = jnp.where(i == 0, 0.0, halo)
    ext_sc[0:HALO, :] = halo
    ext_sc[HALO:HALO + tm, :] = cur_ref[...]
    gd = cur_ref.shape[1] // len(POOL_WINDOWS)
    count_pos = (pos0 + i * tm + 1 + lax.broadcasted_iota(jnp.int32, (tm, 1), 0)).astype(f32)
    for gi, win in enumerate(POOL_WINDOWS):
        cols = slice(gi * gd, (gi + 1) * gd)
        u = ext_sc[HALO:HALO + tm, cols]
        wsum = u
        for k in range(1, win):
            wsum = wsum + ext_sc[pl.ds(HALO - k, tm), cols]
        mean = wsum / jnp.minimum(count_pos, float(win))
        o_ref[:, cols] = (mean - u).astype(o_ref.dtype)


def pool_mix(u, halo, *, nb, seq, tm, row0, pos0, halo_from_u):
    d = u.shape[1]
    nt = seq // tm
    blk0 = row0 // tm
    assert row0 % tm == 0 and (halo_from_u or nt == 1)
    if halo_from_u:
        assert tm % HALO == 0
        hb = tm // HALO
        halo_spec = pl.BlockSpec((HALO, d), lambda b, i: (jnp.maximum((blk0 + b * nt + i) * hb - 1, 0), 0))
        halo = u
    else:
        halo_spec = pl.BlockSpec((HALO, d), lambda b, i: (b, 0))
    kern = functools.partial(_pool_mix_kernel, tm=tm, pos0=pos0, zero_first_halo=halo_from_u)
    return pl.pallas_call(
        kern,
        grid=(nb, nt),
        in_specs=[pl.BlockSpec((tm, d), lambda b, i: (blk0 + b * nt + i, 0)), halo_spec],
        out_specs=pl.BlockSpec((tm, d), lambda b, i: (b * nt + i, 0)),
        out_shape=jax.ShapeDtypeStruct((nb * seq, d), bf16),
        scratch_shapes=[pltpu.VMEM((HALO + tm, d), f32)],
        compiler_params=_cparams(("parallel", "arbitrary"), 40),
        name="pool_mix",
    )(u, halo)


def _rope_tables(pos):
    inv_freq = 1.0 / (ROPE_THETA ** (jnp.arange(0, QK_ROPE, 2, dtype=f32) / QK_ROPE))
    ang = pos.astype(f32)[:, None] * inv_freq[None, :]
    cos, sin = jnp.cos(ang), jnp.sin(ang)
    zeros = jnp.zeros((pos.shape[0], ROPE_PAD - QK_ROPE), f32)
    return (jnp.concatenate([cos, cos, zeros], axis=1), jnp.concatenate([-sin, sin, zeros], axis=1))


def _swap_halves(w):
    half = w.shape[-1] // 2
    return jnp.concatenate([w[..., half:], w[..., :half]], axis=-1)


def _pad_last(w, n):
    return jnp.pad(w, [(0, 0)] * (w.ndim - 1) + [(0, n - w.shape[-1])])


def kernel(x_prompt, x_sample, cache_mla_latent, cache_mla_rope, state_ssm_conv, state_ssm, state_pool, page_table, norm_mix_g, norm_ffn_g, norm_final_g, w_in, q_norm_g, kv_norm_g, w_uq, w_ukv, conv_w, conv_b, dt_bias, a_log, d_skip, ssm_norm_g, w_out, w_pool, pool_scale, w_gate, w_up, w_down):
    bp, lp, d_model = x_prompt.shape
    bs, ls, _ = x_sample.shape
    mp, ms_rows = bp * lp, bs * ls
    li = 0

    wi = w_in[li]
    o_kv = Q_LORA
    o_kr = o_kv + KV_LORA
    o_z = o_kr + QK_ROPE
    o_xbc = o_z + SSM_D_INNER
    o_dt = o_xbc + CONV_DIM
    w_kr = wi[:, o_kr:o_z]
    w_in_r = jnp.concatenate(
        [wi[:, o_xbc:o_dt], wi[:, o_z:o_xbc], wi[:, :o_kv], wi[:, o_kv:o_kr],
         _pad_last(w_kr, ROPE_PAD), _pad_last(_swap_halves(w_kr), ROPE_PAD),
         _pad_last(wi[:, o_dt:], 2 * LANES)], axis=1).astype(bf16)
    assert w_in_r.shape[1] == PROJ_W
    wq = w_uq[li]
    wq_nope = wq[:, :, :QK_NOPE].transpose(1, 0, 2).astype(bf16)
    wq_rope = wq[:, :, QK_NOPE:].transpose(1, 0, 2)
    wq_r = _pad_last(wq_rope, ROPE_PAD).astype(bf16)
    wq_rr = _pad_last(_swap_halves(wq_rope), ROPE_PAD).astype(bf16)
    wkv = w_ukv[li]
    w_uk = wkv[:, :, :QK_NOPE].transpose(1, 2, 0).astype(bf16)
    w_uv = wkv[:, :, QK_NOPE:].transpose(1, 0, 2).astype(bf16)
    w_out_b = w_out[li].astype(bf16)
    w_pool_b = w_pool[0].astype(bf16)

    pos = jnp.concatenate([jnp.tile(jnp.arange(lp), bp), jnp.tile(PAST_LEN + jnp.arange(ls), bs)])
    cos_t, sin_t = _rope_tables(pos)

    x0 = jnp.concatenate([x_prompt.reshape(mp, d_model), x_sample.reshape(ms_rows, d_model)], axis=0)

    u = rmsnorm(x0, norm_mix_g[0], tm=256, out_dtype=bf16)
    proj = matmul(u, w_in_r, tm=1024, tn=256, out_dtype=f32, name="in_proj")
    ckv, kpe, kcat = kv_prep(proj, cos_t, sin_t, kv_norm_g[li], tm=512)

    q_w = (q_norm_g[li], wq_nope, wq_r, wq_rr, w_uk)
    q_p = q_prep(proj, cos_t, sin_t, *q_w, rows=mp, row_blk0=0, tm=512, out_dtype=bf16)
    q_s = q_prep(proj, cos_t, sin_t, *q_w, rows=ms_rows, row_blk0=mp // 512, tm=512, out_dtype=f32)
    o_p = attn_prompt(q_p, kcat[:mp].reshape(bp, lp, QK_CAT), w_uv, nb=bp, seq=lp, tq=128, tk=512, row_chunks=4)
    o_s = attn_sample(page_table, q_s, ckv, kpe, cache_mla_latent, jnp.swapaxes(cache_mla_rope, 2, 3), w_uv,
                      li=li, row0=mp, t_new=ls, ch=16, n_sub=4)

    hist_p = jnp.zeros((bp, CONV_HALO, CONV_DIM), f32)
    hist_s = jnp.pad(state_ssm_conv[li], ((0, 0), (CONV_HALO - (CONV_W - 1), 0), (0, 0)))
    h0_p = jnp.zeros((bp, SSM_GROUPS, GROUP_P, SSM_D_STATE), f32)
    h0_s = state_ssm[li].reshape(bs, SSM_GROUPS, GROUP_P, SSM_D_STATE)
    ssd_w = (conv_w[li], conv_b[li], dt_bias[li], a_log[li], d_skip[li], ssm_norm_g[li])
    y_p, h_p = ssd_mixer(proj, hist_p, h0_p, *ssd_w, nb=bp, seq=lp, q=SSD_CHUNK, row0=0,
                         mxu_dtype=bf16, out_dtype=bf16)
    y_s, h_s = ssd_mixer(proj, hist_s, h0_s, *ssd_w, nb=bs, seq=ls, q=ls, row0=mp,
                         mxu_dtype=f32, out_dtype=f32)

    mix = jnp.concatenate(
        [jnp.concatenate([o_p.reshape(mp, -1), y_p], axis=1),
         jnp.concatenate([o_s.reshape(ms_rows, -1), y_s], axis=1).astype(bf16)], axis=0)
    x1 = matmul(mix, w_out_b, tm=1024, tn=256, out_dtype=f32, residual=x0, name="out_proj")

    def ffn(x, layer):
        uf = rmsnorm(x, norm_ffn_g[layer], tm=256, out_dtype=bf16)
        hid = ffn_up(uf, w_gate[layer].astype(bf16), w_up[layer].astype(bf16), tm=1024, tn=256)
        return matmul(hid, w_down[layer].astype(bf16), tm=512, tn=256, out_dtype=f32, residual=x,
                      name="ffn_down")

    x2 = ffn(x1, 0)

    u1 = rmsnorm(x2, norm_mix_g[1], tm=256, out_dtype=f32)
    halo_s = jnp.pad(state_pool[0], ((0, 0), (HALO - POOL_HIST, 0), (0, 0))).reshape(bs * HALO, d_model)
    mix_p = pool_mix(u1, None, nb=bp, seq=lp, tm=256, row0=0, pos0=0, halo_from_u=True)
    mix_s = pool_mix(u1, halo_s, nb=bs, seq=ls, tm=ls, row0=mp, pos0=PAST_LEN, halo_from_u=False)
    x3 = pool_matmul(jnp.concatenate([mix_p, mix_s], axis=0), w_pool_b, pool_scale[0], x2, tm=1024, tn=512)
    x4 = ffn(x3, 1)

    y_prompt = rmsnorm(x4, norm_final_g, tm=256, out_dtype=f32, rows=mp).reshape(bp, lp, d_model)
    y_sample = rmsnorm(x4, norm_final_g, tm=256, out_dtype=f32, rows=ms_rows,
                       row_blk0=mp // 256).reshape(bs, ls, d_model)

    u1_p_tail = u1[:mp].reshape(bp, lp, d_model)[:, lp - POOL_HIST:]
    u1_s = u1[mp:].reshape(bs, ls, d_model)
    new_pool_s = jnp.concatenate([state_pool[0], u1_s], axis=1)[:, -POOL_HIST:]
    conv_raw = proj[:, COL_XBC:COL_XBC + CONV_DIM]
    new_conv_p = conv_raw[:mp].reshape(bp, lp, CONV_DIM)[:, lp - (CONV_W - 1):]
    new_conv_s = jnp.concatenate([state_ssm_conv[li], conv_raw[mp:].reshape(bs, ls, CONV_DIM)],
                                 axis=1)[:, -(CONV_W - 1):]
    st = lambda h, nb: h.reshape(nb, SSM_HEADS, SSM_HEAD_DIM, SSM_D_STATE)
    return (y_prompt, y_sample,
            ckv[:mp].reshape(1, bp, lp, KV_LORA), kpe[:mp, :QK_ROPE].reshape(1, bp, lp, QK_ROPE),
            ckv[mp:].reshape(1, bs, ls, KV_LORA), kpe[mp:, :QK_ROPE].reshape(1, bs, ls, QK_ROPE),
            new_conv_p[None], new_conv_s[None], st(h_p, bp)[None], st(h_s, bs)[None],
            u1_p_tail[None], new_pool_s[None])
```

```python
import functools

import jax
import jax.numpy as jnp
from jax import lax
from jax.experimental import pallas as pl
from jax.experimental.pallas import tpu as pltpu

f32 = jnp.float32
bf16 = jnp.bfloat16

RMS_EPS = 1e-6
ROPE_THETA = 10000.0
PAST_LEN = 16384
PAGE_SIZE = 128
MLA_HEADS = 16
QK_NOPE = 128
QK_ROPE = 64
V_DIM = 128
Q_LORA = 1024
KV_LORA = 512
ATTN_SCALE = (QK_NOPE + QK_ROPE) ** -0.5
SSM_D_INNER = 2048
SSM_HEAD_DIM = 64
SSM_HEADS = 32
SSM_GROUPS = 8
SSM_D_STATE = 128
CONV_W = 4
CONV_DIM = SSM_D_INNER + 2 * SSM_GROUPS * SSM_D_STATE
SSD_CHUNK = 128
POOL_WINDOWS = (2, 4, 8, 16)
POOL_HIST = 15

LANES = 128
SUBLANES = 8
GROUP_P = SSM_D_INNER // SSM_GROUPS
HEADS_PER_GROUP = SSM_HEADS // SSM_GROUPS
ROPE_PAD = LANES
QK_CAT = KV_LORA + ROPE_PAD
HALO = 16
CONV_HALO = SUBLANES

COL_XBC = 0
COL_Z = CONV_DIM
COL_Q = COL_Z + SSM_D_INNER
COL_KV = COL_Q + Q_LORA
COL_KR = COL_KV + KV_LORA
COL_KRR = COL_KR + ROPE_PAD
COL_DT = COL_KRR + ROPE_PAD
PROJ_W = COL_DT + 2 * LANES


def _dot(a, b):
    return jnp.dot(a, b, preferred_element_type=f32)


def _dot_nt(a, b):
    return lax.dot_general(a, b, (((1,), (1,)), ((), ())), preferred_element_type=f32)


def _sigmoid(x):
    return 1.0 / (1.0 + jnp.exp(-x))


def _split3(x):
    hi = x.astype(bf16).astype(f32)
    r1 = x - hi
    mid = r1.astype(bf16).astype(f32)
    lo = (r1 - mid).astype(bf16).astype(f32)
    return hi, mid, lo


def _cparams(sem, vmem_mib):
    return pltpu.CompilerParams(dimension_semantics=sem, vmem_limit_bytes=vmem_mib << 20)


def _norm_kernel(x_ref, g_ref, o_ref):
    x = x_ref[...].astype(f32)
    ms = jnp.mean(x * x, axis=-1, keepdims=True)
    o_ref[...] = (x * lax.rsqrt(ms + RMS_EPS) * g_ref[...]).astype(o_ref.dtype)


def rmsnorm(x, g, *, tm, out_dtype, rows=None, row_blk0=0):
    rows = x.shape[0] if rows is None else rows
    width = x.shape[1]
    return pl.pallas_call(
        _norm_kernel,
        grid=(rows // tm,),
        in_specs=[pl.BlockSpec((tm, width), lambda i: (i + row_blk0, 0)),
                  pl.BlockSpec((1, width), lambda i: (0, 0))],
        out_specs=pl.BlockSpec((tm, width), lambda i: (i, 0)),
        out_shape=jax.ShapeDtypeStruct((rows, width), out_dtype),
        compiler_params=_cparams(("parallel",), 40),
        name="rmsnorm",
    )(x, g.reshape(1, width).astype(f32))


def _two_group_specs(n_first, n_second, shape):
    return (pl.BlockSpec(shape, lambda i, *_: (jnp.minimum(i, n_first - 1), 0)),
            pl.BlockSpec(shape, lambda i, *_: (jnp.clip(i - n_first, 0, n_second - 1), 0)))


def _norm2_kernel(xp_ref, xs_ref, g_ref, o_ref, *, n_first):
    i = pl.program_id(0)

    @pl.when(i < n_first)
    def _():
        _norm_kernel(xp_ref, g_ref, o_ref)

    @pl.when(i >= n_first)
    def _():
        _norm_kernel(xs_ref, g_ref, o_ref)


def rmsnorm_stacked(xp, xs, g, *, tm, out_dtype):
    (mp, width), ms = xp.shape, xs.shape[0]
    n_first, n_second = mp // tm, ms // tm
    return pl.pallas_call(
        functools.partial(_norm2_kernel, n_first=n_first),
        grid=(n_first + n_second,),
        in_specs=[*_two_group_specs(n_first, n_second, (tm, width)),
                  pl.BlockSpec((1, width), lambda i: (0, 0))],
        out_specs=pl.BlockSpec((tm, width), lambda i: (i, 0)),
        out_shape=jax.ShapeDtypeStruct((mp + ms, width), out_dtype),
        compiler_params=_cparams(("arbitrary",), 40),
        name="rmsnorm_stacked",
    )(xp, xs, g.reshape(1, width).astype(f32))


def _mm_kernel(a_ref, w_ref, o_ref):
    o_ref[...] = _dot(a_ref[...], w_ref[...]).astype(o_ref.dtype)


def matmul(a, w, *, tm, tn, out_dtype, name="matmul"):
    m, k = a.shape
    n = w.shape[1]
    return pl.pallas_call(
        _mm_kernel,
        grid=(m // tm, n // tn),
        in_specs=[pl.BlockSpec((tm, k), lambda i, j: (i, 0)),
                  pl.BlockSpec((k, tn), lambda i, j: (0, j))],
        out_specs=pl.BlockSpec((tm, tn), lambda i, j: (i, j)),
        out_shape=jax.ShapeDtypeStruct((m, n), out_dtype),
        compiler_params=_cparams(("parallel", "arbitrary"), 56),
        name=name,
    )(a, w)


def _out_proj_kernel(op_ref, os_ref, yp_ref, ys_ref, w_ref, rp_ref, rs_ref, o_ref, *, n_first):
    i = pl.program_id(0)
    k1 = op_ref.shape[1]

    def project(a1_ref, a2_ref, r_ref):
        acc = _dot(a1_ref[...].astype(bf16), w_ref[:k1, :]) + _dot(a2_ref[...].astype(bf16), w_ref[k1:, :])
        o_ref[...] = acc + r_ref[...]

    @pl.when(i < n_first)
    def _():
        project(op_ref, yp_ref, rp_ref)

    @pl.when(i >= n_first)
    def _():
        project(os_ref, ys_ref, rs_ref)


def out_proj(o_p, y_p, o_s, y_s, w, res_p, res_s, *, tm, tn):
    (mp, k1), k2, ms = o_p.shape, y_p.shape[1], o_s.shape[0]
    n = w.shape[1]
    n_first, n_second = mp // tm, ms // tm
    res_p_spec = pl.BlockSpec((tm, tn), lambda i, j: (jnp.minimum(i, n_first - 1), j))
    res_s_spec = pl.BlockSpec((tm, tn), lambda i, j: (jnp.clip(i - n_first, 0, n_second - 1), j))
    return pl.pallas_call(
        functools.partial(_out_proj_kernel, n_first=n_first),
        grid=(n_first + n_second, n // tn),
        in_specs=[*_two_group_specs(n_first, n_second, (tm, k1)), *_two_group_specs(n_first, n_second, (tm, k2)),
                  pl.BlockSpec((k1 + k2, tn), lambda i, j: (0, j)), res_p_spec, res_s_spec],
        out_specs=pl.BlockSpec((tm, tn), lambda i, j: (i, j)),
        out_shape=jax.ShapeDtypeStruct((mp + ms, n), f32),
        compiler_params=_cparams(("arbitrary", "arbitrary"), 56),
        name="out_proj",
    )(o_p, o_s, y_p, y_s, w, res_p, res_s)


def _ffn_up_kernel(a_ref, wg_ref, wu_ref, o_ref):
    a = a_ref[...]
    g = _dot(a, wg_ref[0].astype(bf16))
    u = _dot(a, wu_ref[0].astype(bf16))
    o_ref[...] = (g * _sigmoid(g) * u).astype(o_ref.dtype)


def ffn_up(a, wg, wu, layer, *, tm, tn):
    m, k = a.shape
    n = wg.shape[2]
    wspec = pl.BlockSpec((1, k, tn), lambda i, j: (layer, 0, j))
    return pl.pallas_call(
        _ffn_up_kernel,
        grid=(m // tm, n // tn),
        in_specs=[pl.BlockSpec((tm, k), lambda i, j: (i, 0)), wspec, wspec],
        out_specs=pl.BlockSpec((tm, tn), lambda i, j: (i, j)),
        out_shape=jax.ShapeDtypeStruct((m, n), bf16),
        compiler_params=_cparams(("parallel", "arbitrary"), 56),
        name="ffn_up",
    )(a, wg, wu)


def _ffn_down_kernel(a_ref, w_ref, r_ref, o_ref):
    o_ref[...] = _dot(a_ref[...], w_ref[0]) + r_ref[...]


def ffn_down(a, w, layer, residual, *, tm, tn):
    m, k = a.shape
    n = w.shape[2]
    return pl.pallas_call(
        _ffn_down_kernel,
        grid=(m // tm, n // tn),
        in_specs=[pl.BlockSpec((tm, k), lambda i, j: (i, 0)),
                  pl.BlockSpec((1, k, tn), lambda i, j: (layer, 0, j)),
                  pl.BlockSpec((tm, tn), lambda i, j: (i, j))],
        out_specs=pl.BlockSpec((tm, tn), lambda i, j: (i, j)),
        out_shape=jax.ShapeDtypeStruct((m, n), f32),
        compiler_params=_cparams(("parallel", "arbitrary"), 56),
        name="ffn_down",
    )(a, w, residual)


def _pool_mm_kernel(ap_ref, as_ref, w_ref, s_ref, r_ref, o_ref, *, n_first):
    i = pl.program_id(0)

    def project(a_ref):
        o_ref[...] = _dot(a_ref[...], w_ref[0]) * s_ref[...] + r_ref[...]

    @pl.when(i < n_first)
    def _():
        project(ap_ref)

    @pl.when(i >= n_first)
    def _():
        project(as_ref)


def pool_matmul(mix_p, mix_s, w, scale, residual, *, tm, tn):
    (mp, d), ms = mix_p.shape, mix_s.shape[0]
    ng, gd, _ = w.shape
    nj = gd // tn
    n_first, n_second = mp // tm, ms // tm
    return pl.pallas_call(
        functools.partial(_pool_mm_kernel, n_first=n_first),
        grid=(n_first + n_second, ng, nj),
        in_specs=[pl.BlockSpec((tm, gd), lambda i, g, j: (jnp.minimum(i, n_first - 1), g)),
                  pl.BlockSpec((tm, gd), lambda i, g, j: (jnp.clip(i - n_first, 0, n_second - 1), g)),
                  pl.BlockSpec((1, gd, tn), lambda i, g, j: (g, 0, j)),
                  pl.BlockSpec((1, tn), lambda i, g, j: (0, g * nj + j)),
                  pl.BlockSpec((tm, tn), lambda i, g, j: (i, g * nj + j))],
        out_specs=pl.BlockSpec((tm, tn), lambda i, g, j: (i, g * nj + j)),
        out_shape=jax.ShapeDtypeStruct((mp + ms, d), f32),
        compiler_params=_cparams(("arbitrary", "arbitrary", "arbitrary"), 40),
        name="pool_matmul",
    )(mix_p, mix_s, w, scale.reshape(1, d).astype(f32), residual)


def _kv_prep_kernel(kv_ref, kr_ref, krr_ref, cos_ref, sin_ref, g_ref, ckv_ref, kpe_ref, kcat_ref):
    x = kv_ref[...]
    ms = jnp.mean(x * x, axis=-1, keepdims=True)
    ckv = x * lax.rsqrt(ms + RMS_EPS) * g_ref[...]
    kpe = kr_ref[...] * cos_ref[...] + krr_ref[...] * sin_ref[...]
    ckv_ref[...] = ckv
    kpe_ref[...] = kpe
    kcat_ref[:, :KV_LORA] = ckv.astype(bf16)
    kcat_ref[:, KV_LORA:] = kpe.astype(bf16)


def kv_prep(proj, cos_t, sin_t, kv_norm_g, *, tm):
    m = proj.shape[0]
    row = lambda w, cb: pl.BlockSpec((tm, w), lambda i: (i, cb))
    return pl.pallas_call(
        _kv_prep_kernel,
        grid=(m // tm,),
        in_specs=[row(KV_LORA, COL_KV // KV_LORA), row(ROPE_PAD, COL_KR // ROPE_PAD),
                  row(ROPE_PAD, COL_KRR // ROPE_PAD), row(ROPE_PAD, 0), row(ROPE_PAD, 0),
                  pl.BlockSpec((1, KV_LORA), lambda i: (0, 0))],
        out_specs=[row(KV_LORA, 0), row(ROPE_PAD, 0), row(QK_CAT, 0)],
        out_shape=[jax.ShapeDtypeStruct((m, KV_LORA), f32), jax.ShapeDtypeStruct((m, ROPE_PAD), f32),
                   jax.ShapeDtypeStruct((m, QK_CAT), bf16)],
        compiler_params=_cparams(("parallel",), 32),
        name="kv_prep",
    )(proj, proj, proj, cos_t, sin_t, kv_norm_g.reshape(1, KV_LORA).astype(f32))


def _q_prep_kernel(cq_ref, g_ref, wn_ref, wr_ref, wrr_ref, wuk_ref, cos_ref, sin_ref, q_ref, cqn_sc):
    @pl.when(pl.program_id(1) == 0)
    def _():
        x = cq_ref[...]
        ms = jnp.mean(x * x, axis=-1, keepdims=True)
        cqn_sc[...] = (x * lax.rsqrt(ms + RMS_EPS) * g_ref[...]).astype(bf16)

    cqn = cqn_sc[...]
    q_nope = _dot(cqn, wn_ref[0]).astype(bf16)
    q_r = _dot(cqn, wr_ref[0])
    q_rr = _dot(cqn, wrr_ref[0])
    for e in range(wuk_ref.shape[0]):
        q_lat = _dot(q_nope[:, e * QK_NOPE:(e + 1) * QK_NOPE], wuk_ref[e])
        cols = slice(e * ROPE_PAD, (e + 1) * ROPE_PAD)
        q_pe = q_r[:, cols] * cos_ref[...] + q_rr[:, cols] * sin_ref[...]
        q_ref[e, :, :KV_LORA] = (q_lat * ATTN_SCALE).astype(q_ref.dtype)
        q_ref[e, :, KV_LORA:] = (q_pe * ATTN_SCALE).astype(q_ref.dtype)


def _side_by_side(w, hps):
    nh, k, n = w.shape
    return w.reshape(nh // hps, hps, k, n).transpose(0, 2, 1, 3).reshape(nh // hps, k, hps * n)


def q_prep(proj, cos_t, sin_t, q_norm_g, wn, wr, wrr, wuk, *, rows, row_blk0, tm, out_dtype, hps=2):
    nh = wn.shape[0]
    hw = lambda a, b: pl.BlockSpec((1, a, hps * b), lambda i, h: (h, 0, 0))
    return pl.pallas_call(
        _q_prep_kernel,
        grid=(rows // tm, nh // hps),
        in_specs=[pl.BlockSpec((tm, Q_LORA), lambda i, h: (i + row_blk0, COL_Q // Q_LORA)),
                  pl.BlockSpec((1, Q_LORA), lambda i, h: (0, 0)),
                  hw(Q_LORA, QK_NOPE), hw(Q_LORA, ROPE_PAD), hw(Q_LORA, ROPE_PAD),
                  pl.BlockSpec((hps, QK_NOPE, KV_LORA), lambda i, h: (h, 0, 0)),
                  pl.BlockSpec((tm, ROPE_PAD), lambda i, h: (i + row_blk0, 0)),
                  pl.BlockSpec((tm, ROPE_PAD), lambda i, h: (i + row_blk0, 0))],
        out_specs=pl.BlockSpec((hps, tm, QK_CAT), lambda i, h: (h, i, 0)),
        out_shape=jax.ShapeDtypeStruct((nh, rows, QK_CAT), out_dtype),
        scratch_shapes=[pltpu.VMEM((tm, Q_LORA), bf16)],
        compiler_params=_cparams(("parallel", "arbitrary"), 40),
        name="q_prep",
    )(proj, q_norm_g.reshape(1, Q_LORA).astype(f32), _side_by_side(wn, hps), _side_by_side(wr, hps),
      _side_by_side(wrr, hps), wuk, cos_t, sin_t)


def _attn_prompt_kernel(q_ref, k_ref, wuv_ref, o_ref, m_sc, l_sc, acc_sc, *, tq, tk, row_chunks):
    nh = q_ref.shape[0]
    hpc = nh // row_chunks
    rc = hpc * tq
    qb = pl.program_id(1)
    m_sc[...] = jnp.full(m_sc.shape, -1e30, f32)
    l_sc[...] = jnp.zeros(l_sc.shape, f32)
    acc_sc[...] = jnp.zeros(acc_sc.shape, f32)

    def step(kb, masked):
        k = k_ref[0, pl.ds(pl.multiple_of(kb * tk, tk), tk), :]
        v = k[:, :KV_LORA]
        for r in range(row_chunks):
            rs = slice(r * rc, (r + 1) * rc)
            q = q_ref[r * hpc:(r + 1) * hpc].reshape(rc, QK_CAT)
            s = _dot_nt(q, k)
            if masked:
                qpos = qb * tq + lax.broadcasted_iota(jnp.int32, (hpc, tq, tk), 1).reshape(rc, tk)
                kpos = kb * tk + lax.broadcasted_iota(jnp.int32, (rc, tk), 1)
                s = jnp.where(kpos <= qpos, s, -jnp.inf)
            m_prev = m_sc[rs]
            m_new = jnp.maximum(m_prev, jnp.max(s, axis=-1, keepdims=True))
            alpha = jnp.exp(m_prev - m_new)
            p = jnp.exp(s - m_new)
            l_sc[rs] = alpha * l_sc[rs] + jnp.sum(p, axis=-1, keepdims=True)
            acc_sc[rs] = alpha * acc_sc[rs] + _dot(p.astype(bf16), v)
            m_sc[rs] = m_new

    n_full = (qb * tq) // tk

    def body(kb, carry):
        step(kb, False)
        return carry

    lax.fori_loop(0, n_full, body, 0)
    step(n_full, True)

    o = (acc_sc[...] / l_sc[...]).astype(bf16).reshape(nh, tq, KV_LORA)
    for h in range(nh):
        o_ref[0, :, h * V_DIM:(h + 1) * V_DIM] = _dot(o[h], wuv_ref[h]).astype(o_ref.dtype)


def attn_prompt(q, kcat, wuv, *, nb, seq, tq, tk, row_chunks):
    nh = q.shape[0]
    assert tk % tq == 0 and seq % tk == 0 and nh % row_chunks == 0
    nqb = seq // tq
    rows = nh * tq
    kern = functools.partial(_attn_prompt_kernel, tq=tq, tk=tk, row_chunks=row_chunks)
    return pl.pallas_call(
        kern,
        grid=(nb, nqb),
        in_specs=[pl.BlockSpec((nh, tq, QK_CAT), lambda b, i: (0, b * nqb + i, 0)),
                  pl.BlockSpec((1, seq, QK_CAT), lambda b, i: (b, 0, 0)),
                  pl.BlockSpec((nh, KV_LORA, V_DIM), lambda b, i: (0, 0, 0))],
        out_specs=pl.BlockSpec((1, tq, nh * V_DIM), lambda b, i: (b, i, 0)),
        out_shape=jax.ShapeDtypeStruct((nb, seq, nh * V_DIM), bf16),
        scratch_shapes=[pltpu.VMEM((rows, 1), f32), pltpu.VMEM((rows, 1), f32),
                        pltpu.VMEM((rows, KV_LORA), f32)],
        compiler_params=_cparams(("parallel", "arbitrary"), 48),
        name="attn_prompt",
    )(q, kcat, wuv)


def _attn_sample_kernel(pt_ref, q_ref, slat_ref, spe_ref, wuv_ref, clat_hbm, crope_hbm, o_ref,
                        lat_buf, rope_buf, sem, m_sc, l_sc, acc_sc, q_sc, *, li, ch, n_sub, t_new):
    nh = q_ref.shape[0]
    rows = nh * t_new
    n_chunks = pt_ref.shape[1] // ch
    ps = ch // n_sub
    b = pl.program_id(0)
    nb = pl.num_programs(0)

    def page_copies(pid, slot, j):
        return (pltpu.make_async_copy(clat_hbm.at[li, pid], lat_buf.at[slot, j], sem.at[0, slot]),
                pltpu.make_async_copy(crope_hbm.at[li, pid], rope_buf.at[slot, j], sem.at[1, slot]))

    def start_chunk(bb, c, slot):
        for j in range(ch):
            for cp in page_copies(pt_ref[bb, c * ch + j], slot, j):
                cp.start()

    def wait_chunk(slot):
        for j in range(ch):
            for cp in page_copies(0, slot, j):
                cp.wait()

    @pl.when(b == 0)
    def _():
        start_chunk(0, 0, 0)

    q = q_ref[...].reshape(rows, QK_CAT).astype(bf16)
    q_sc[...] = q
    pad = 2 * SUBLANES - t_new
    ks = jnp.concatenate([jnp.concatenate([slat_ref[...], spe_ref[...]], axis=1),
                          jnp.zeros((pad, QK_CAT), f32)], axis=0).astype(bf16)
    s = _dot_nt(q, ks)
    tpos = lax.broadcasted_iota(jnp.int32, (nh, t_new, 2 * SUBLANES), 1).reshape(rows, 2 * SUBLANES)
    kpos = lax.broadcasted_iota(jnp.int32, (rows, 2 * SUBLANES), 1)
    s = jnp.where(kpos <= tpos, s, -jnp.inf)
    m0 = jnp.maximum(jnp.max(s, axis=-1, keepdims=True), -1e30)
    p = jnp.exp(s - m0)
    m_sc[...] = m0
    l_sc[...] = jnp.sum(p, axis=-1, keepdims=True)
    acc_sc[...] = _dot(p.astype(bf16), ks[:, :KV_LORA])

    def softmax_part(slot, sub):
        k = lat_buf[slot, sub * ps:(sub + 1) * ps].reshape(ps * PAGE_SIZE, KV_LORA).astype(bf16)
        kr_t = jnp.concatenate([rope_buf[slot, sub * ps + j] for j in range(ps)], axis=1).astype(bf16)
        s = _dot_nt(q_sc[:, :KV_LORA], k) + _dot(q_sc[:, KV_LORA:KV_LORA + QK_ROPE], kr_t)
        m_s = jnp.max(s, axis=-1, keepdims=True)
        p = jnp.exp(s - m_s)
        return m_s, jnp.sum(p, axis=-1, keepdims=True), _dot(p.astype(bf16), k)

    def merge(parts):
        m_old = m_sc[...]
        m_new = m_old
        for m_s, _, _ in parts:
            m_new = jnp.maximum(m_new, m_s)
        alpha = jnp.exp(m_old - m_new)
        l = alpha * l_sc[...]
        acc = alpha * acc_sc[...]
        for m_s, l_s, o_s in parts:
            w = jnp.exp(m_s - m_new)
            l = l + w * l_s
            acc = acc + w * o_s
        m_sc[...] = m_new
        l_sc[...] = l
        acc_sc[...] = acc

    def chunk(c, carry):
        slot = c & 1
        wait_chunk(slot)
        wrap = c + 1 == n_chunks
        bb = jnp.where(wrap, b + 1, b)
        cc = jnp.where(wrap, 0, c + 1)

        @pl.when(bb < nb)
        def _():
            start_chunk(bb, cc, 1 - slot)

        merge([softmax_part(slot, sub) for sub in range(n_sub)])
        return carry

    lax.fori_loop(0, n_chunks, chunk, 0)

    o = (acc_sc[...] / l_sc[...]).astype(bf16).astype(f32).reshape(nh, t_new, KV_LORA)
    for h in range(nh):
        o_ref[0, :, h * V_DIM:(h + 1) * V_DIM] = _dot(o[h], wuv_ref[h].astype(f32))


def attn_sample(page_table, q, ckv, kpe, cache_lat, cache_rope_t, wuv, *, li, row0, t_new, ch, n_sub):
    nh = q.shape[0]
    nb, n_pages = page_table.shape
    assert n_pages % (2 * ch) == 0 and ch % n_sub == 0 and row0 % t_new == 0
    rows = nh * t_new
    blk0 = row0 // t_new
    kern = functools.partial(_attn_sample_kernel, li=li, ch=ch, n_sub=n_sub, t_new=t_new)
    grid_spec = pltpu.PrefetchScalarGridSpec(
        num_scalar_prefetch=1,
        grid=(nb,),
        in_specs=[pl.BlockSpec((nh, t_new, QK_CAT), lambda b, pt: (0, b, 0)),
                  pl.BlockSpec((t_new, KV_LORA), lambda b, pt: (blk0 + b, 0)),
                  pl.BlockSpec((t_new, ROPE_PAD), lambda b, pt: (blk0 + b, 0)),
                  pl.BlockSpec((nh, KV_LORA, V_DIM), lambda b, pt: (0, 0, 0)),
                  pl.BlockSpec(memory_space=pl.ANY),
                  pl.BlockSpec(memory_space=pl.ANY)],
        out_specs=pl.BlockSpec((1, t_new, nh * V_DIM), lambda b, pt: (b, 0, 0)),
        scratch_shapes=[pltpu.VMEM((2, ch, PAGE_SIZE, KV_LORA), f32),
                        pltpu.VMEM((2, ch, QK_ROPE, PAGE_SIZE), f32),
                        pltpu.SemaphoreType.DMA((2, 2)),
                        pltpu.VMEM((rows, 1), f32), pltpu.VMEM((rows, 1), f32),
                        pltpu.VMEM((rows, KV_LORA), f32),
                        pltpu.VMEM((rows, QK_CAT), bf16)],
    )
    return pl.pallas_call(
        kern,
        grid_spec=grid_spec,
        out_shape=jax.ShapeDtypeStruct((nb, t_new, nh * V_DIM), f32),
        compiler_params=_cparams(("arbitrary",), 32),
        name="attn_sample",
    )(page_table, q, ckv, kpe, wuv, cache_lat, cache_rope_t)


def _expand_heads(cols, n):
    q = cols.shape[0]
    lane_head = lax.broadcasted_iota(jnp.int32, (q, GROUP_P), 1) // SSM_HEAD_DIM
    out = jnp.broadcast_to(cols[:, 0:1], (q, GROUP_P))
    for r in range(1, n):
        out = jnp.where(lane_head == r, cols[:, r:r + 1], out)
    return out


def _ssd_kernel(xbc_ref, z_ref, dt_ref, hist_ref, h0_ref, cw_ref, cb_ref, dtb_ref, a_ref, dsk_ref, ng_ref,
                y_ref, hout_ref, ext_sc, h_sc, y_sc, *, q, mxu_dtype):
    c = pl.program_id(1)
    cast = lambda v: v.astype(mxu_dtype)

    @pl.when(c == 0)
    def _():
        ext_sc[CONV_HALO - (CONV_W - 1):CONV_HALO, :] = hist_ref[0]
        h_sc[...] = h0_ref[0]

    ext_sc[CONV_HALO:CONV_HALO + q, :] = xbc_ref[...]
    conv = cb_ref[...]
    for k in range(CONV_W):
        conv = conv + ext_sc[pl.ds(CONV_HALO - (CONV_W - 1) + k, q), :] * cw_ref[k:k + 1, :]
    ext_sc[0:CONV_HALO, :] = ext_sc[q:q + CONV_HALO, :]
    act = conv * _sigmoid(conv)

    dt_in = dt_ref[...] + dtb_ref[...]
    dt = jnp.maximum(dt_in, 0.0) + jnp.log1p(jnp.exp(-jnp.abs(dt_in)))
    dta = dt * a_ref[...]
    ri = lax.broadcasted_iota(jnp.int32, (q, q), 0)
    ci = lax.broadcasted_iota(jnp.int32, (q, q), 1)
    tril = ci <= ri
    trif = tril.astype(f32)
    eye_l = (lax.broadcasted_iota(jnp.int32, (LANES, LANES), 0)
             == lax.broadcasted_iota(jnp.int32, (LANES, LANES), 1)).astype(f32)
    eye_p = (lax.broadcasted_iota(jnp.int32, (GROUP_P, GROUP_P), 0)
             == lax.broadcasted_iota(jnp.int32, (GROUP_P, GROUP_P), 1)).astype(mxu_dtype)
    a_cum = sum(_dot(trif, part) for part in _split3(dta))
    a_cum_t = sum(_dot_nt(eye_l, part) for part in _split3(a_cum))
    dt_t = sum(_dot_nt(eye_l, part) for part in _split3(dt))
    a_last = a_cum[q - 1:q, :]
    decay = jnp.exp(a_last - a_cum) * dt
    e_acum = jnp.exp(a_cum)
    e_last_t = jnp.exp(a_cum_t[:, q - 1:q])
    lane_head = lax.broadcasted_iota(jnp.int32, (q, GROUP_P), 1) // SSM_HEAD_DIM

    for g in range(SSM_GROUPS):
        h_lo = g * HEADS_PER_GROUP
        bg = cast(act[:, SSM_D_INNER + g * SSM_D_STATE:SSM_D_INNER + (g + 1) * SSM_D_STATE])
        cg = cast(act[:, SSM_D_INNER + (SSM_GROUPS + g) * SSM_D_STATE:
                      SSM_D_INNER + (SSM_GROUPS + g + 1) * SSM_D_STATE])
        xg = act[:, g * GROUP_P:(g + 1) * GROUP_P]
        xg_m = cast(xg)
        cb = _dot_nt(cg, bg)
        hg = h_sc[g]
        y_off = _dot_nt(cg, cast(hg))
        y_diag = jnp.zeros((q, GROUP_P), f32)
        for r in range(HEADS_PER_GROUP):
            h = h_lo + r
            seg = a_cum[:, h:h + 1] - a_cum_t[h:h + 1, :]
            w = cb * jnp.exp(jnp.where(tril, seg, -jnp.inf)) * dt_t[h:h + 1, :]
            y_diag = jnp.where(lane_head == r, _dot(cast(w), xg_m), y_diag)
        e_g = _expand_heads(e_acum[:, h_lo:h_lo + HEADS_PER_GROUP], HEADS_PER_GROUP)
        d_g = _expand_heads(decay[:, h_lo:h_lo + HEADS_PER_GROUP], HEADS_PER_GROUP)
        y_sc[:, g * GROUP_P:(g + 1) * GROUP_P] = (
            y_diag + y_off * e_g + dsk_ref[:, g * GROUP_P:(g + 1) * GROUP_P] * xg)
        xdec_t = cast(_dot_nt(eye_p, cast(xg * d_g)))
        states = _dot(xdec_t, bg)
        cdec = jnp.concatenate(
            [jnp.broadcast_to(e_last_t[h_lo + r:h_lo + r + 1, :], (SSM_HEAD_DIM, SSM_D_STATE))
             for r in range(HEADS_PER_GROUP)], axis=0)
        h_sc[g] = hg * cdec + states

    gated = y_sc[...] * (z_ref[...] * _sigmoid(z_ref[...]))
    ms = jnp.mean(gated * gated, axis=-1, keepdims=True)
    y_ref[...] = (gated * lax.rsqrt(ms + RMS_EPS) * ng_ref[...]).astype(y_ref.dtype)

    @pl.when(c == pl.num_programs(1) - 1)
    def _():
        hout_ref[0] = h_sc[...]


def ssd_mixer(proj, hist, h0, conv_w, conv_b, dt_bias, a_log, d_skip, norm_g, *, nb, seq, q, row0,
              mxu_dtype, out_dtype):
    assert seq % q == 0 and row0 % q == 0 and q >= CONV_HALO
    nc = seq // q
    blk0 = row0 // q
    pad_l = lambda v: jnp.pad(v.astype(f32), (0, LANES - v.shape[0])).reshape(1, LANES)
    a = -jnp.exp(a_log.astype(f32))
    row = lambda w, cb: pl.BlockSpec((q, w), lambda b, c: (blk0 + b * nc + c, cb))
    const = lambda r, w: pl.BlockSpec((r, w), lambda b, c: (0, 0))
    kern = functools.partial(_ssd_kernel, q=q, mxu_dtype=mxu_dtype)
    state_spec = pl.BlockSpec((1, SSM_GROUPS, GROUP_P, SSM_D_STATE), lambda b, c: (b, 0, 0, 0))
    return pl.pallas_call(
        kern,
        grid=(nb, nc),
        in_specs=[row(CONV_DIM, COL_XBC // CONV_DIM), row(SSM_D_INNER, COL_Z // SSM_D_INNER),
                  row(LANES, COL_DT // LANES),
                  pl.BlockSpec((1, CONV_W - 1, CONV_DIM), lambda b, c: (b, 0, 0)), state_spec,
                  const(CONV_W, CONV_DIM), const(1, CONV_DIM), const(1, LANES), const(1, LANES),
                  const(1, SSM_D_INNER), const(1, SSM_D_INNER)],
        out_specs=[pl.BlockSpec((q, SSM_D_INNER), lambda b, c: (b * nc + c, 0)), state_spec],
        out_shape=[jax.ShapeDtypeStruct((nb * seq, SSM_D_INNER), out_dtype),
                   jax.ShapeDtypeStruct((nb, SSM_GROUPS, GROUP_P, SSM_D_STATE), f32)],
        scratch_shapes=[pltpu.VMEM((q + CONV_HALO, CONV_DIM), f32),
                        pltpu.VMEM((SSM_GROUPS, GROUP_P, SSM_D_STATE), f32),
                        pltpu.VMEM((q, SSM_D_INNER), f32)],
        compiler_params=_cparams(("parallel", "arbitrary"), 40),
        name="ssd_mixer",
    )(proj, proj, proj, hist, h0, conv_w.astype(f32), conv_b.reshape(1, CONV_DIM).astype(f32),
      pad_l(dt_bias), pad_l(a), jnp.repeat(d_skip.astype(f32), SSM_HEAD_DIM).reshape(1, SSM_D_INNER),
      norm_g.reshape(1, SSM_D_INNER).astype(f32))


def _pool_mix_kernel(cur_ref, halo_ref, o_ref, ext_sc, *, tm, pos0, zero_first_halo):
    i = pl.program_id(1)
    if zero_first_halo:
        ext_sc[0:HALO, :] = jnp.where(i == 0, 0.0, halo_ref[...])
    else:
        ext_sc[HALO - POOL_HIST:HALO, :] = halo_ref[0]
    ext_sc[HALO:HALO + tm, :] = cur_ref[...]
    gd = cur_ref.shape[1] // len(POOL_WINDOWS)
    count_pos = (pos0 + i * tm + 1 + lax.broadcasted_iota(jnp.int32, (tm, 1), 0)).astype(f32)
    for gi, win in enumerate(POOL_WINDOWS):
        cols = slice(gi * gd, (gi + 1) * gd)
        u = ext_sc[HALO:HALO + tm, cols]
        wsum = u
        for k in range(1, win):
            wsum = wsum + ext_sc[pl.ds(HALO - k, tm), cols]
        mean = wsum / jnp.minimum(count_pos, float(win))
        o_ref[:, cols] = (mean - u).astype(o_ref.dtype)


def pool_mix(u, halo, *, nb, seq, tm, row0, pos0, halo_from_u):
    d = u.shape[1]
    nt = seq // tm
    blk0 = row0 // tm
    assert row0 % tm == 0 and (halo_from_u or nt == 1)
    if halo_from_u:
        assert tm % HALO == 0
        hb = tm // HALO
        halo_spec = pl.BlockSpec((HALO, d), lambda b, i: (jnp.maximum((blk0 + b * nt + i) * hb - 1, 0), 0))
        halo = u
    else:
        halo_spec = pl.BlockSpec((1, POOL_HIST, d), lambda b, i: (b, 0, 0))
    kern = functools.partial(_pool_mix_kernel, tm=tm, pos0=pos0, zero_first_halo=halo_from_u)
    return pl.pallas_call(
        kern,
        grid=(nb, nt),
        in_specs=[pl.BlockSpec((tm, d), lambda b, i: (blk0 + b * nt + i, 0)), halo_spec],
        out_specs=pl.BlockSpec((tm, d), lambda b, i: (b * nt + i, 0)),
        out_shape=jax.ShapeDtypeStruct((nb * seq, d), bf16),
        scratch_shapes=[pltpu.VMEM((HALO + tm, d), f32)],
        compiler_params=_cparams(("parallel", "arbitrary"), 40),
        name="pool_mix",
    )(u, halo)


def _rope_tables(pos):
    inv_freq = 1.0 / (ROPE_THETA ** (jnp.arange(0, QK_ROPE, 2, dtype=f32) / QK_ROPE))
    ang = pos.astype(f32)[:, None] * inv_freq[None, :]
    cos, sin = jnp.cos(ang), jnp.sin(ang)
    zeros = jnp.zeros((pos.shape[0], ROPE_PAD - QK_ROPE), f32)
    return (jnp.concatenate([cos, cos, zeros], axis=1), jnp.concatenate([-sin, sin, zeros], axis=1))


def _swap_halves(w):
    half = w.shape[-1] // 2
    return jnp.concatenate([w[..., half:], w[..., :half]], axis=-1)


def _pad_last(w, n):
    return jnp.pad(w, [(0, 0)] * (w.ndim - 1) + [(0, n - w.shape[-1])])


def kernel(x_prompt, x_sample, cache_mla_latent, cache_mla_rope, state_ssm_conv, state_ssm, state_pool, page_table, norm_mix_g, norm_ffn_g, norm_final_g, w_in, q_norm_g, kv_norm_g, w_uq, w_ukv, conv_w, conv_b, dt_bias, a_log, d_skip, ssm_norm_g, w_out, w_pool, pool_scale, w_gate, w_up, w_down):
    bp, lp, d_model = x_prompt.shape
    bs, ls, _ = x_sample.shape
    mp, ms_rows = bp * lp, bs * ls
    li = 0

    wi = w_in[li]
    o_kv = Q_LORA
    o_kr = o_kv + KV_LORA
    o_z = o_kr + QK_ROPE
    o_xbc = o_z + SSM_D_INNER
    o_dt = o_xbc + CONV_DIM
    w_kr = wi[:, o_kr:o_z]
    w_in_r = jnp.concatenate(
        [wi[:, o_xbc:o_dt], wi[:, o_z:o_xbc], wi[:, :o_kv], wi[:, o_kv:o_kr],
         _pad_last(w_kr, ROPE_PAD), _pad_last(_swap_halves(w_kr), ROPE_PAD),
         _pad_last(wi[:, o_dt:], 2 * LANES)], axis=1).astype(bf16)
    assert w_in_r.shape[1] == PROJ_W
    wq = w_uq[li]
    wq_nope = wq[:, :, :QK_NOPE].transpose(1, 0, 2).astype(bf16)
    wq_rope = wq[:, :, QK_NOPE:].transpose(1, 0, 2)
    wq_r = _pad_last(wq_rope, ROPE_PAD).astype(bf16)
    wq_rr = _pad_last(_swap_halves(wq_rope), ROPE_PAD).astype(bf16)
    wkv = w_ukv[li]
    w_uk = wkv[:, :, :QK_NOPE].transpose(1, 2, 0).astype(bf16)
    w_uv = wkv[:, :, QK_NOPE:].transpose(1, 0, 2).astype(bf16)
    w_out_b = w_out[li].astype(bf16)
    w_pool_b = w_pool[0].astype(bf16)
    w_down_b = w_down.astype(bf16)

    pos = jnp.concatenate([jnp.tile(jnp.arange(lp), bp), jnp.tile(PAST_LEN + jnp.arange(ls), bs)])
    cos_t, sin_t = _rope_tables(pos)

    xp2 = x_prompt.reshape(mp, d_model)
    xs2 = x_sample.reshape(ms_rows, d_model)

    u = rmsnorm_stacked(xp2, xs2, norm_mix_g[0], tm=256, out_dtype=bf16)
    proj = matmul(u, w_in_r, tm=1024, tn=256, out_dtype=f32, name="in_proj")
    ckv, kpe, kcat = kv_prep(proj, cos_t, sin_t, kv_norm_g[li], tm=512)

    q_w = (q_norm_g[li], wq_nope, wq_r, wq_rr, w_uk)
    q_p = q_prep(proj, cos_t, sin_t, *q_w, rows=mp, row_blk0=0, tm=1024, out_dtype=bf16)
    q_s = q_prep(proj, cos_t, sin_t, *q_w, rows=ms_rows, row_blk0=mp // 1024, tm=1024, out_dtype=f32)
    o_p = attn_prompt(q_p, kcat[:mp].reshape(bp, lp, QK_CAT), w_uv, nb=bp, seq=lp, tq=128, tk=512, row_chunks=4)
    o_s = attn_sample(page_table, q_s, ckv, kpe, cache_mla_latent, jnp.swapaxes(cache_mla_rope, 2, 3), w_uv,
                      li=li, row0=mp, t_new=ls, ch=16, n_sub=4)

    hist_p = jnp.zeros((bp, CONV_W - 1, CONV_DIM), f32)
    h0_p = jnp.zeros((bp, SSM_GROUPS, GROUP_P, SSM_D_STATE), f32)
    h0_s = state_ssm[li].reshape(bs, SSM_GROUPS, GROUP_P, SSM_D_STATE)
    ssd_w = (conv_w[li], conv_b[li], dt_bias[li], a_log[li], d_skip[li], ssm_norm_g[li])
    y_p, h_p = ssd_mixer(proj, hist_p, h0_p, *ssd_w, nb=bp, seq=lp, q=SSD_CHUNK, row0=0,
                         mxu_dtype=bf16, out_dtype=bf16)
    y_s, h_s = ssd_mixer(proj, state_ssm_conv[li], h0_s, *ssd_w, nb=bs, seq=ls, q=ls, row0=mp,
                         mxu_dtype=f32, out_dtype=f32)

    x1 = out_proj(o_p.reshape(mp, -1), y_p, o_s.reshape(ms_rows, -1), y_s, w_out_b, xp2, xs2, tm=512, tn=256)

    def ffn(x, layer):
        uf = rmsnorm(x, norm_ffn_g[layer], tm=256, out_dtype=bf16)
        hid = ffn_up(uf, w_gate, w_up, layer, tm=1024, tn=256)
        return ffn_down(hid, w_down_b, layer, x, tm=512, tn=256)

    x2 = ffn(x1, 0)

    u1 = rmsnorm(x2, norm_mix_g[1], tm=256, out_dtype=f32)
    mix_p = pool_mix(u1, None, nb=bp, seq=lp, tm=256, row0=0, pos0=0, halo_from_u=True)
    mix_s = pool_mix(u1, state_pool[0], nb=bs, seq=ls, tm=ls, row0=mp, pos0=PAST_LEN, halo_from_u=False)
    x3 = pool_matmul(mix_p, mix_s, w_pool_b, pool_scale[0], x2, tm=1024, tn=512)
    x4 = ffn(x3, 1)

    y_prompt = rmsnorm(x4, norm_final_g, tm=256, out_dtype=f32, rows=mp).reshape(bp, lp, d_model)
    y_sample = rmsnorm(x4, norm_final_g, tm=256, out_dtype=f32, rows=ms_rows,
                       row_blk0=mp // 256).reshape(bs, ls, d_model)

    tail = lambda v, n, width: jnp.stack([v[(b + 1) * lp - n:(b + 1) * lp, :width] for b in range(bp)])
    u1_s = u1[mp:].reshape(bs, ls, d_model)
    new_pool_s = jnp.concatenate([state_pool[0], u1_s], axis=1)[:, -POOL_HIST:]
    new_conv_s = jnp.concatenate([state_ssm_conv[li], proj[mp:, :CONV_DIM].reshape(bs, ls, CONV_DIM)],
                                 axis=1)[:, -(CONV_W - 1):]
    st = lambda h, nb: h.reshape(nb, SSM_HEADS, SSM_HEAD_DIM, SSM_D_STATE)
    return (y_prompt, y_sample,
            ckv[:mp].reshape(1, bp, lp, KV_LORA), kpe[:mp, :QK_ROPE].reshape(1, bp, lp, QK_ROPE),
            ckv[mp:].reshape(1, bs, ls, KV_LORA), kpe[mp:, :QK_ROPE].reshape(1, bs, ls, QK_ROPE),
            tail(proj, CONV_W - 1, CONV_DIM)[None], new_conv_s[None], st(h_p, bp)[None], st(h_s, bs)[None],
            tail(u1, POOL_HIST, d_model)[None], new_pool_s[None])
```

```python
import functools

import jax
import jax.numpy as jnp
from jax import lax
from jax.experimental import pallas as pl
from jax.experimental.pallas import tpu as pltpu

f32 = jnp.float32
bf16 = jnp.bfloat16

RMS_EPS = 1e-6
ROPE_THETA = 10000.0
PAST_LEN = 16384
PAGE_SIZE = 128
MLA_HEADS = 16
QK_NOPE = 128
QK_ROPE = 64
V_DIM = 128
Q_LORA = 1024
KV_LORA = 512
ATTN_SCALE = (QK_NOPE + QK_ROPE) ** -0.5
SSM_D_INNER = 2048
SSM_HEAD_DIM = 64
SSM_HEADS = 32
SSM_GROUPS = 8
SSM_D_STATE = 128
CONV_W = 4
CONV_DIM = SSM_D_INNER + 2 * SSM_GROUPS * SSM_D_STATE
SSD_CHUNK = 128
POOL_WINDOWS = (2, 4, 8, 16)
POOL_HIST = 15

LANES = 128
SUBLANES = 8
GROUP_P = SSM_D_INNER // SSM_GROUPS
HEADS_PER_GROUP = SSM_HEADS // SSM_GROUPS
ROPE_PAD = LANES
QK_CAT = KV_LORA + ROPE_PAD
HALO = 16
CONV_HALO = SUBLANES

COL_XBC = 0
COL_Z = CONV_DIM
COL_Q = COL_Z + SSM_D_INNER
COL_KV = COL_Q + Q_LORA
COL_KR = COL_KV + KV_LORA
COL_KRR = COL_KR + ROPE_PAD
COL_DT = COL_KRR + ROPE_PAD
PROJ_W = COL_DT + 2 * LANES


def _dot(a, b):
    return jnp.dot(a, b, preferred_element_type=f32)


def _dot_nt(a, b):
    return lax.dot_general(a, b, (((1,), (1,)), ((), ())), preferred_element_type=f32)


def _sigmoid(x):
    return 1.0 / (1.0 + jnp.exp(-x))


def _split3(x):
    hi = x.astype(bf16).astype(f32)
    r1 = x - hi
    mid = r1.astype(bf16).astype(f32)
    lo = (r1 - mid).astype(bf16).astype(f32)
    return hi, mid, lo


def _cparams(sem, vmem_mib):
    return pltpu.CompilerParams(dimension_semantics=sem, vmem_limit_bytes=vmem_mib << 20)


def _norm_kernel(x_ref, g_ref, o_ref):
    x = x_ref[...].astype(f32)
    ms = jnp.mean(x * x, axis=-1, keepdims=True)
    o_ref[...] = (x * lax.rsqrt(ms + RMS_EPS) * g_ref[...]).astype(o_ref.dtype)


def rmsnorm(x, g, *, tm, out_dtype, rows=None, row_blk0=0):
    rows = x.shape[0] if rows is None else rows
    width = x.shape[1]
    return pl.pallas_call(
        _norm_kernel,
        grid=(rows // tm,),
        in_specs=[pl.BlockSpec((tm, width), lambda i: (i + row_blk0, 0)),
                  pl.BlockSpec((1, width), lambda i: (0, 0))],
        out_specs=pl.BlockSpec((tm, width), lambda i: (i, 0)),
        out_shape=jax.ShapeDtypeStruct((rows, width), out_dtype),
        compiler_params=_cparams(("parallel",), 40),
        name="rmsnorm",
    )(x, g.reshape(1, width).astype(f32))


def _two_group_specs(n_first, n_second, shape):
    return (pl.BlockSpec(shape, lambda i, *_: (jnp.minimum(i, n_first - 1), 0)),
            pl.BlockSpec(shape, lambda i, *_: (jnp.clip(i - n_first, 0, n_second - 1), 0)))


def _norm2_kernel(xp_ref, xs_ref, g_ref, o_ref, *, n_first):
    i = pl.program_id(0)

    @pl.when(i < n_first)
    def _():
        _norm_kernel(xp_ref, g_ref, o_ref)

    @pl.when(i >= n_first)
    def _():
        _norm_kernel(xs_ref, g_ref, o_ref)


def rmsnorm_stacked(xp, xs, g, *, tm, out_dtype):
    (mp, width), ms = xp.shape, xs.shape[0]
    n_first, n_second = mp // tm, ms // tm
    return pl.pallas_call(
        functools.partial(_norm2_kernel, n_first=n_first),
        grid=(n_first + n_second,),
        in_specs=[*_two_group_specs(n_first, n_second, (tm, width)),
                  pl.BlockSpec((1, width), lambda i: (0, 0))],
        out_specs=pl.BlockSpec((tm, width), lambda i: (i, 0)),
        out_shape=jax.ShapeDtypeStruct((mp + ms, width), out_dtype),
        compiler_params=_cparams(("arbitrary",), 40),
        name="rmsnorm_stacked",
    )(xp, xs, g.reshape(1, width).astype(f32))


def _mm_kernel(a_ref, w_ref, o_ref):
    o_ref[...] = _dot(a_ref[...], w_ref[...]).astype(o_ref.dtype)


def matmul(a, w, *, tm, tn, out_dtype, name="matmul"):
    m, k = a.shape
    n = w.shape[1]
    return pl.pallas_call(
        _mm_kernel,
        grid=(m // tm, n // tn),
        in_specs=[pl.BlockSpec((tm, k), lambda i, j: (i, 0)),
                  pl.BlockSpec((k, tn), lambda i, j: (0, j))],
        out_specs=pl.BlockSpec((tm, tn), lambda i, j: (i, j)),
        out_shape=jax.ShapeDtypeStruct((m, n), out_dtype),
        compiler_params=_cparams(("parallel", "arbitrary"), 56),
        name=name,
    )(a, w)


def _out_proj_kernel(op_ref, os_ref, yp_ref, ys_ref, w_ref, rp_ref, rs_ref, o_ref, *, n_first):
    i = pl.program_id(0)
    k1 = op_ref.shape[1]

    def project(a1_ref, a2_ref, r_ref):
        acc = _dot(a1_ref[...].astype(bf16), w_ref[:k1, :]) + _dot(a2_ref[...].astype(bf16), w_ref[k1:, :])
        o_ref[...] = acc + r_ref[...]

    @pl.when(i < n_first)
    def _():
        project(op_ref, yp_ref, rp_ref)

    @pl.when(i >= n_first)
    def _():
        project(os_ref, ys_ref, rs_ref)


def out_proj(o_p, y_p, o_s, y_s, w, res_p, res_s, *, tm, tn):
    (mp, k1), k2, ms = o_p.shape, y_p.shape[1], o_s.shape[0]
    n = w.shape[1]
    n_first, n_second = mp // tm, ms // tm
    res_p_spec = pl.BlockSpec((tm, tn), lambda i, j: (jnp.minimum(i, n_first - 1), j))
    res_s_spec = pl.BlockSpec((tm, tn), lambda i, j: (jnp.clip(i - n_first, 0, n_second - 1), j))
    return pl.pallas_call(
        functools.partial(_out_proj_kernel, n_first=n_first),
        grid=(n_first + n_second, n // tn),
        in_specs=[*_two_group_specs(n_first, n_second, (tm, k1)), *_two_group_specs(n_first, n_second, (tm, k2)),
                  pl.BlockSpec((k1 + k2, tn), lambda i, j: (0, j)), res_p_spec, res_s_spec],
        out_specs=pl.BlockSpec((tm, tn), lambda i, j: (i, j)),
        out_shape=jax.ShapeDtypeStruct((mp + ms, n), f32),
        compiler_params=_cparams(("arbitrary", "arbitrary"), 56),
        name="out_proj",
    )(o_p, o_s, y_p, y_s, w, res_p, res_s)


def _ffn_up_kernel(a_ref, wg_ref, wu_ref, o_ref):
    a = a_ref[...]
    g = _dot(a, wg_ref[0].astype(bf16))
    u = _dot(a, wu_ref[0].astype(bf16))
    o_ref[...] = (g * _sigmoid(g) * u).astype(o_ref.dtype)


def ffn_up(a, wg, wu, layer, *, tm, tn):
    m, k = a.shape
    n = wg.shape[2]
    wspec = pl.BlockSpec((1, k, tn), lambda i, j: (layer, 0, j))
    return pl.pallas_call(
        _ffn_up_kernel,
        grid=(m // tm, n // tn),
        in_specs=[pl.BlockSpec((tm, k), lambda i, j: (i, 0)), wspec, wspec],
        out_specs=pl.BlockSpec((tm, tn), lambda i, j: (i, j)),
        out_shape=jax.ShapeDtypeStruct((m, n), bf16),
        compiler_params=_cparams(("parallel", "arbitrary"), 56),
        name="ffn_up",
    )(a, wg, wu)


def _ffn_down_kernel(a_ref, w_ref, r_ref, o_ref):
    o_ref[...] = _dot(a_ref[...], w_ref[0]) + r_ref[...]


def ffn_down(a, w, layer, residual, *, tm, tn):
    m, k = a.shape
    n = w.shape[2]
    return pl.pallas_call(
        _ffn_down_kernel,
        grid=(m // tm, n // tn),
        in_specs=[pl.BlockSpec((tm, k), lambda i, j: (i, 0)),
                  pl.BlockSpec((1, k, tn), lambda i, j: (layer, 0, j)),
                  pl.BlockSpec((tm, tn), lambda i, j: (i, j))],
        out_specs=pl.BlockSpec((tm, tn), lambda i, j: (i, j)),
        out_shape=jax.ShapeDtypeStruct((m, n), f32),
        compiler_params=_cparams(("parallel", "arbitrary"), 56),
        name="ffn_down",
    )(a, w, residual)


def _pool_mm_kernel(ap_ref, as_ref, w_ref, s_ref, r_ref, o_ref, *, n_first):
    i = pl.program_id(0)

    def project(a_ref):
        o_ref[...] = _dot(a_ref[...], w_ref[0]) * s_ref[...] + r_ref[...]

    @pl.when(i < n_first)
    def _():
        project(ap_ref)

    @pl.when(i >= n_first)
    def _():
        project(as_ref)


def pool_matmul(mix_p, mix_s, w, scale, residual, *, tm, tn):
    (mp, d), ms = mix_p.shape, mix_s.shape[0]
    ng, gd, _ = w.shape
    nj = gd // tn
    n_first, n_second = mp // tm, ms // tm
    return pl.pallas_call(
        functools.partial(_pool_mm_kernel, n_first=n_first),
        grid=(n_first + n_second, ng, nj),
        in_specs=[pl.BlockSpec((tm, gd), lambda i, g, j: (jnp.minimum(i, n_first - 1), g)),
                  pl.BlockSpec((tm, gd), lambda i, g, j: (jnp.clip(i - n_first, 0, n_second - 1), g)),
                  pl.BlockSpec((1, gd, tn), lambda i, g, j: (g, 0, j)),
                  pl.BlockSpec((1, tn), lambda i, g, j: (0, g * nj + j)),
                  pl.BlockSpec((tm, tn), lambda i, g, j: (i, g * nj + j))],
        out_specs=pl.BlockSpec((tm, tn), lambda i, g, j: (i, g * nj + j)),
        out_shape=jax.ShapeDtypeStruct((mp + ms, d), f32),
        compiler_params=_cparams(("arbitrary", "arbitrary", "arbitrary"), 40),
        name="pool_matmul",
    )(mix_p, mix_s, w, scale.reshape(1, d).astype(f32), residual)


def _kv_prep_kernel(kv_ref, kr_ref, krr_ref, cos_ref, sin_ref, g_ref, ckv_ref, kpe_ref, kcat_ref):
    x = kv_ref[...]
    ms = jnp.mean(x * x, axis=-1, keepdims=True)
    ckv = x * lax.rsqrt(ms + RMS_EPS) * g_ref[...]
    kpe = kr_ref[...] * cos_ref[...] + krr_ref[...] * sin_ref[...]
    ckv_ref[...] = ckv
    kpe_ref[...] = kpe
    kcat_ref[:, :KV_LORA] = ckv.astype(bf16)
    kcat_ref[:, KV_LORA:] = kpe.astype(bf16)


def kv_prep(proj, cos_t, sin_t, kv_norm_g, *, tm):
    m = proj.shape[0]
    row = lambda w, cb: pl.BlockSpec((tm, w), lambda i: (i, cb))
    return pl.pallas_call(
        _kv_prep_kernel,
        grid=(m // tm,),
        in_specs=[row(KV_LORA, COL_KV // KV_LORA), row(ROPE_PAD, COL_KR // ROPE_PAD),
                  row(ROPE_PAD, COL_KRR // ROPE_PAD), row(ROPE_PAD, 0), row(ROPE_PAD, 0),
                  pl.BlockSpec((1, KV_LORA), lambda i: (0, 0))],
        out_specs=[row(KV_LORA, 0), row(ROPE_PAD, 0), row(QK_CAT, 0)],
        out_shape=[jax.ShapeDtypeStruct((m, KV_LORA), f32), jax.ShapeDtypeStruct((m, ROPE_PAD), f32),
                   jax.ShapeDtypeStruct((m, QK_CAT), bf16)],
        compiler_params=_cparams(("parallel",), 32),
        name="kv_prep",
    )(proj, proj, proj, cos_t, sin_t, kv_norm_g.reshape(1, KV_LORA).astype(f32))


def _q_prep_kernel(cq_ref, g_ref, wn_ref, wr_ref, wrr_ref, wuk_ref, cos_ref, sin_ref, q_ref, cqn_sc):
    @pl.when(pl.program_id(1) == 0)
    def _():
        x = cq_ref[...]
        ms = jnp.mean(x * x, axis=-1, keepdims=True)
        cqn_sc[...] = (x * lax.rsqrt(ms + RMS_EPS) * g_ref[...]).astype(bf16)

    cqn = cqn_sc[...]
    q_nope = _dot(cqn, wn_ref[0]).astype(bf16)
    q_r = _dot(cqn, wr_ref[0])
    q_rr = _dot(cqn, wrr_ref[0])
    for e in range(wuk_ref.shape[0]):
        q_lat = _dot(q_nope[:, e * QK_NOPE:(e + 1) * QK_NOPE], wuk_ref[e])
        cols = slice(e * ROPE_PAD, (e + 1) * ROPE_PAD)
        q_pe = q_r[:, cols] * cos_ref[...] + q_rr[:, cols] * sin_ref[...]
        q_ref[e, :, :KV_LORA] = (q_lat * ATTN_SCALE).astype(q_ref.dtype)
        q_ref[e, :, KV_LORA:] = (q_pe * ATTN_SCALE).astype(q_ref.dtype)


def _side_by_side(w, hps):
    nh, k, n = w.shape
    return w.reshape(nh // hps, hps, k, n).transpose(0, 2, 1, 3).reshape(nh // hps, k, hps * n)


def q_prep(proj, cos_t, sin_t, q_norm_g, wn, wr, wrr, wuk, *, rows, row_blk0, tm, out_dtype, hps=2):
    nh = wn.shape[0]
    hw = lambda a, b: pl.BlockSpec((1, a, hps * b), lambda i, h: (h, 0, 0))
    return pl.pallas_call(
        _q_prep_kernel,
        grid=(rows // tm, nh // hps),
        in_specs=[pl.BlockSpec((tm, Q_LORA), lambda i, h: (i + row_blk0, COL_Q // Q_LORA)),
                  pl.BlockSpec((1, Q_LORA), lambda i, h: (0, 0)),
                  hw(Q_LORA, QK_NOPE), hw(Q_LORA, ROPE_PAD), hw(Q_LORA, ROPE_PAD),
                  pl.BlockSpec((hps, QK_NOPE, KV_LORA), lambda i, h: (h, 0, 0)),
                  pl.BlockSpec((tm, ROPE_PAD), lambda i, h: (i + row_blk0, 0)),
                  pl.BlockSpec((tm, ROPE_PAD), lambda i, h: (i + row_blk0, 0))],
        out_specs=pl.BlockSpec((hps, tm, QK_CAT), lambda i, h: (h, i, 0)),
        out_shape=jax.ShapeDtypeStruct((nh, rows, QK_CAT), out_dtype),
        scratch_shapes=[pltpu.VMEM((tm, Q_LORA), bf16)],
        compiler_params=_cparams(("parallel", "arbitrary"), 40),
        name="q_prep",
    )(proj, q_norm_g.reshape(1, Q_LORA).astype(f32), _side_by_side(wn, hps), _side_by_side(wr, hps),
      _side_by_side(wrr, hps), wuk, cos_t, sin_t)


def _attn_prompt_kernel(q_ref, k_ref, wuv_ref, o_ref, m_sc, l_sc, acc_sc, *, tq, tk, row_chunks):
    nh = q_ref.shape[0]
    hpc = nh // row_chunks
    rc = hpc * tq
    qb = pl.program_id(1)
    m_sc[...] = jnp.full(m_sc.shape, -1e30, f32)
    l_sc[...] = jnp.zeros(l_sc.shape, f32)
    acc_sc[...] = jnp.zeros(acc_sc.shape, f32)

    def step(kb, masked, width=tk):
        k = k_ref[0, pl.ds(pl.multiple_of(kb * tk, tk), width), :]
        v = k[:, :KV_LORA]
        for r in range(row_chunks):
            rs = slice(r * rc, (r + 1) * rc)
            q = q_ref[r * hpc:(r + 1) * hpc].reshape(rc, QK_CAT)
            s = _dot_nt(q, k)
            if masked:
                qpos = qb * tq + lax.broadcasted_iota(jnp.int32, (hpc, tq, width), 1).reshape(rc, width)
                kpos = kb * tk + lax.broadcasted_iota(jnp.int32, (rc, width), 1)
                s = jnp.where(kpos <= qpos, s, -jnp.inf)
            m_prev = m_sc[rs]
            m_new = jnp.maximum(m_prev, jnp.max(s, axis=-1, keepdims=True))
            alpha = jnp.exp(m_prev - m_new)
            p = jnp.exp(s - m_new)
            l_sc[rs] = alpha * l_sc[rs] + jnp.sum(p, axis=-1, keepdims=True)
            acc_sc[rs] = alpha * acc_sc[rs] + _dot(p.astype(bf16), v)
            m_sc[rs] = m_new

    n_full = (qb * tq) // tk

    def body(kb, carry):
        step(kb, False)
        return carry

    lax.fori_loop(0, n_full, body, 0)
    if tk % (2 * tq) == 0:
        first_half = (qb + 1) * tq - n_full * tk <= tk // 2

        @pl.when(first_half)
        def _():
            step(n_full, True, tk // 2)

        @pl.when(jnp.logical_not(first_half))
        def _():
            step(n_full, True)
    else:
        step(n_full, True)

    o = (acc_sc[...] / l_sc[...]).astype(bf16).reshape(nh, tq, KV_LORA)
    for h in range(nh):
        o_ref[0, :, h * V_DIM:(h + 1) * V_DIM] = _dot(o[h], wuv_ref[h]).astype(o_ref.dtype)


def attn_prompt(q, kcat, wuv, *, nb, seq, tq, tk, row_chunks):
    nh = q.shape[0]
    assert tk % tq == 0 and seq % tk == 0 and nh % row_chunks == 0
    nqb = seq // tq
    rows = nh * tq
    kern = functools.partial(_attn_prompt_kernel, tq=tq, tk=tk, row_chunks=row_chunks)
    return pl.pallas_call(
        kern,
        grid=(nb, nqb),
        in_specs=[pl.BlockSpec((nh, tq, QK_CAT), lambda b, i: (0, b * nqb + i, 0)),
                  pl.BlockSpec((1, seq, QK_CAT), lambda b, i: (b, 0, 0)),
                  pl.BlockSpec((nh, KV_LORA, V_DIM), lambda b, i: (0, 0, 0))],
        out_specs=pl.BlockSpec((1, tq, nh * V_DIM), lambda b, i: (b, i, 0)),
        out_shape=jax.ShapeDtypeStruct((nb, seq, nh * V_DIM), bf16),
        scratch_shapes=[pltpu.VMEM((rows, 1), f32), pltpu.VMEM((rows, 1), f32),
                        pltpu.VMEM((rows, KV_LORA), f32)],
        compiler_params=_cparams(("parallel", "arbitrary"), 48),
        name="attn_prompt",
    )(q, kcat, wuv)


def _attn_sample_kernel(pt_ref, q_ref, slat_ref, spe_ref, wuv_ref, clat_hbm, crope_hbm, o_ref,
                        lat_buf, rope_buf, sem, m_sc, l_sc, acc_sc, q_sc, *, li, ch, n_sub, t_new):
    nh = q_ref.shape[0]
    rows = nh * t_new
    n_chunks = pt_ref.shape[1] // ch
    ps = ch // n_sub
    b = pl.program_id(0)
    nb = pl.num_programs(0)

    def page_copies(pid, slot, j):
        return (pltpu.make_async_copy(clat_hbm.at[li, pid], lat_buf.at[slot, j], sem.at[0, slot]),
                pltpu.make_async_copy(crope_hbm.at[li, pid], rope_buf.at[slot, j], sem.at[1, slot]))

    def start_chunk(bb, c, slot):
        for j in range(ch):
            for cp in page_copies(pt_ref[bb, c * ch + j], slot, j):
                cp.start()

    def wait_chunk(slot):
        for j in range(ch):
            for cp in page_copies(0, slot, j):
                cp.wait()

    n_slots = lat_buf.shape[0]
    ahead = n_slots - 1

    def chunk_at(g):
        return b + g // n_chunks, g % n_chunks

    @pl.when(b == 0)
    def _():
        for g in range(ahead):
            start_chunk(g // n_chunks, g % n_chunks, g % n_slots)

    q = q_ref[...].reshape(rows, QK_CAT).astype(bf16)
    q_sc[...] = q
    pad = 2 * SUBLANES - t_new
    ks = jnp.concatenate([jnp.concatenate([slat_ref[...], spe_ref[...]], axis=1),
                          jnp.zeros((pad, QK_CAT), f32)], axis=0).astype(bf16)
    s = _dot_nt(q, ks)
    tpos = lax.broadcasted_iota(jnp.int32, (nh, t_new, 2 * SUBLANES), 1).reshape(rows, 2 * SUBLANES)
    kpos = lax.broadcasted_iota(jnp.int32, (rows, 2 * SUBLANES), 1)
    s = jnp.where(kpos <= tpos, s, -jnp.inf)
    m0 = jnp.maximum(jnp.max(s, axis=-1, keepdims=True), -1e30)
    p = jnp.exp(s - m0)
    m_sc[...] = m0
    l_sc[...] = jnp.sum(p, axis=-1, keepdims=True)
    acc_sc[...] = _dot(p.astype(bf16), ks[:, :KV_LORA])

    def softmax_part(slot, sub):
        k = lat_buf[slot, sub * ps:(sub + 1) * ps].reshape(ps * PAGE_SIZE, KV_LORA).astype(bf16)
        kr_t = jnp.concatenate([rope_buf[slot, sub * ps + j] for j in range(ps)], axis=1).astype(bf16)
        s = _dot_nt(q_sc[:, :KV_LORA], k) + _dot(q_sc[:, KV_LORA:KV_LORA + QK_ROPE], kr_t)
        m_s = jnp.max(s, axis=-1, keepdims=True)
        p = jnp.exp(s - m_s)
        return m_s, jnp.sum(p, axis=-1, keepdims=True), _dot(p.astype(bf16), k)

    def merge(parts):
        m_old = m_sc[...]
        m_new = m_old
        for m_s, _, _ in parts:
            m_new = jnp.maximum(m_new, m_s)
        alpha = jnp.exp(m_old - m_new)
        l = alpha * l_sc[...]
        acc = alpha * acc_sc[...]
        for m_s, l_s, o_s in parts:
            w = jnp.exp(m_s - m_new)
            l = l + w * l_s
            acc = acc + w * o_s
        m_sc[...] = m_new
        l_sc[...] = l
        acc_sc[...] = acc

    def chunk(c, carry):
        g = b * n_chunks + c
        slot = g % n_slots
        wait_chunk(slot)
        bb, cc = chunk_at(c + ahead)

        @pl.when(bb < nb)
        def _():
            start_chunk(bb, cc, (g + ahead) % n_slots)

        merge([softmax_part(slot, sub) for sub in range(n_sub)])
        return carry

    lax.fori_loop(0, n_chunks, chunk, 0)

    o = (acc_sc[...] / l_sc[...]).astype(bf16).astype(f32).reshape(nh, t_new, KV_LORA)
    for h in range(nh):
        o_ref[0, :, h * V_DIM:(h + 1) * V_DIM] = _dot(o[h], wuv_ref[h].astype(f32))


def attn_sample(page_table, q, ckv, kpe, cache_lat, cache_rope_t, wuv, *, li, row0, t_new, ch, n_sub, n_slots):
    nh = q.shape[0]
    nb, n_pages = page_table.shape
    assert n_pages % ch == 0 and ch % n_sub == 0 and row0 % t_new == 0
    assert 2 <= n_slots <= nb * (n_pages // ch)
    rows = nh * t_new
    blk0 = row0 // t_new
    kern = functools.partial(_attn_sample_kernel, li=li, ch=ch, n_sub=n_sub, t_new=t_new)
    grid_spec = pltpu.PrefetchScalarGridSpec(
        num_scalar_prefetch=1,
        grid=(nb,),
        in_specs=[pl.BlockSpec((nh, t_new, QK_CAT), lambda b, pt: (0, b, 0)),
                  pl.BlockSpec((t_new, KV_LORA), lambda b, pt: (blk0 + b, 0)),
                  pl.BlockSpec((t_new, ROPE_PAD), lambda b, pt: (blk0 + b, 0)),
                  pl.BlockSpec((nh, KV_LORA, V_DIM), lambda b, pt: (0, 0, 0)),
                  pl.BlockSpec(memory_space=pl.ANY),
                  pl.BlockSpec(memory_space=pl.ANY)],
        out_specs=pl.BlockSpec((1, t_new, nh * V_DIM), lambda b, pt: (b, 0, 0)),
        scratch_shapes=[pltpu.VMEM((n_slots, ch, PAGE_SIZE, KV_LORA), f32),
                        pltpu.VMEM((n_slots, ch, QK_ROPE, PAGE_SIZE), f32),
                        pltpu.SemaphoreType.DMA((2, n_slots)),
                        pltpu.VMEM((rows, 1), f32), pltpu.VMEM((rows, 1), f32),
                        pltpu.VMEM((rows, KV_LORA), f32),
                        pltpu.VMEM((rows, QK_CAT), bf16)],
    )
    return pl.pallas_call(
        kern,
        grid_spec=grid_spec,
        out_shape=jax.ShapeDtypeStruct((nb, t_new, nh * V_DIM), f32),
        compiler_params=_cparams(("arbitrary",), 40),
        name="attn_sample",
    )(page_table, q, ckv, kpe, wuv, cache_lat, cache_rope_t)


def _expand_heads(cols, n):
    q = cols.shape[0]
    lane_head = lax.broadcasted_iota(jnp.int32, (q, GROUP_P), 1) // SSM_HEAD_DIM
    out = jnp.broadcast_to(cols[:, 0:1], (q, GROUP_P))
    for r in range(1, n):
        out = jnp.where(lane_head == r, cols[:, r:r + 1], out)
    return out


def _ssd_kernel(xbc_ref, z_ref, dt_ref, hist_ref, h0_ref, cw_ref, cb_ref, dtb_ref, a_ref, dsk_ref, ng_ref,
                y_ref, hout_ref, ext_sc, h_sc, y_sc, *, q, mxu_dtype):
    c = pl.program_id(1)
    cast = lambda v: v.astype(mxu_dtype)

    @pl.when(c == 0)
    def _():
        ext_sc[CONV_HALO - (CONV_W - 1):CONV_HALO, :] = hist_ref[0]
        h_sc[...] = h0_ref[0]

    ext_sc[CONV_HALO:CONV_HALO + q, :] = xbc_ref[...]
    conv = cb_ref[...]
    for k in range(CONV_W):
        conv = conv + ext_sc[pl.ds(CONV_HALO - (CONV_W - 1) + k, q), :] * cw_ref[k:k + 1, :]
    ext_sc[0:CONV_HALO, :] = ext_sc[q:q + CONV_HALO, :]
    act = conv * _sigmoid(conv)

    dt_in = dt_ref[...] + dtb_ref[...]
    dt = jnp.maximum(dt_in, 0.0) + jnp.log1p(jnp.exp(-jnp.abs(dt_in)))
    dta = dt * a_ref[...]
    ri = lax.broadcasted_iota(jnp.int32, (q, q), 0)
    ci = lax.broadcasted_iota(jnp.int32, (q, q), 1)
    tril = ci <= ri
    trif = tril.astype(f32)
    eye_l = (lax.broadcasted_iota(jnp.int32, (LANES, LANES), 0)
             == lax.broadcasted_iota(jnp.int32, (LANES, LANES), 1)).astype(f32)
    eye_p = (lax.broadcasted_iota(jnp.int32, (GROUP_P, GROUP_P), 0)
             == lax.broadcasted_iota(jnp.int32, (GROUP_P, GROUP_P), 1)).astype(mxu_dtype)
    a_cum = sum(_dot(trif, part) for part in _split3(dta))
    a_cum_t = sum(_dot_nt(eye_l, part) for part in _split3(a_cum))
    dt_t = sum(_dot_nt(eye_l, part) for part in _split3(dt))
    a_last = a_cum[q - 1:q, :]
    decay = jnp.exp(a_last - a_cum) * dt
    e_acum = jnp.exp(a_cum)
    e_last_t = jnp.exp(a_cum_t[:, q - 1:q])
    lane_head = lax.broadcasted_iota(jnp.int32, (q, GROUP_P), 1) // SSM_HEAD_DIM

    for g in range(SSM_GROUPS):
        h_lo = g * HEADS_PER_GROUP
        bg = cast(act[:, SSM_D_INNER + g * SSM_D_STATE:SSM_D_INNER + (g + 1) * SSM_D_STATE])
        cg = cast(act[:, SSM_D_INNER + (SSM_GROUPS + g) * SSM_D_STATE:
                      SSM_D_INNER + (SSM_GROUPS + g + 1) * SSM_D_STATE])
        xg = act[:, g * GROUP_P:(g + 1) * GROUP_P]
        xg_m = cast(xg)
        cb = _dot_nt(cg, bg)
        hg = h_sc[g]
        y_off = _dot_nt(cg, cast(hg))
        y_diag = jnp.zeros((q, GROUP_P), f32)
        for r in range(HEADS_PER_GROUP):
            h = h_lo + r
            seg = a_cum[:, h:h + 1] - a_cum_t[h:h + 1, :]
            w = cb * jnp.exp(jnp.where(tril, seg, -jnp.inf)) * dt_t[h:h + 1, :]
            y_diag = jnp.where(lane_head == r, _dot(cast(w), xg_m), y_diag)
        e_g = _expand_heads(e_acum[:, h_lo:h_lo + HEADS_PER_GROUP], HEADS_PER_GROUP)
        d_g = _expand_heads(decay[:, h_lo:h_lo + HEADS_PER_GROUP], HEADS_PER_GROUP)
        y_sc[:, g * GROUP_P:(g + 1) * GROUP_P] = (
            y_diag + y_off * e_g + dsk_ref[:, g * GROUP_P:(g + 1) * GROUP_P] * xg)
        xdec_t = cast(_dot_nt(eye_p, cast(xg * d_g)))
        states = _dot(xdec_t, bg)
        cdec = jnp.concatenate(
            [jnp.broadcast_to(e_last_t[h_lo + r:h_lo + r + 1, :], (SSM_HEAD_DIM, SSM_D_STATE))
             for r in range(HEADS_PER_GROUP)], axis=0)
        h_sc[g] = hg * cdec + states

    gated = y_sc[...] * (z_ref[...] * _sigmoid(z_ref[...]))
    ms = jnp.mean(gated * gated, axis=-1, keepdims=True)
    y_ref[...] = (gated * lax.rsqrt(ms + RMS_EPS) * ng_ref[...]).astype(y_ref.dtype)

    @pl.when(c == pl.num_programs(1) - 1)
    def _():
        hout_ref[0] = h_sc[...]


def ssd_mixer(proj, hist, h0, conv_w, conv_b, dt_bias, a_log, d_skip, norm_g, *, nb, seq, q, row0,
              mxu_dtype, out_dtype):
    assert seq % q == 0 and row0 % q == 0 and q >= CONV_HALO
    nc = seq // q
    blk0 = row0 // q
    pad_l = lambda v: jnp.pad(v.astype(f32), (0, LANES - v.shape[0])).reshape(1, LANES)
    a = -jnp.exp(a_log.astype(f32))
    row = lambda w, cb: pl.BlockSpec((q, w), lambda b, c: (blk0 + b * nc + c, cb))
    const = lambda r, w: pl.BlockSpec((r, w), lambda b, c: (0, 0))
    kern = functools.partial(_ssd_kernel, q=q, mxu_dtype=mxu_dtype)
    state_spec = pl.BlockSpec((1, SSM_GROUPS, GROUP_P, SSM_D_STATE), lambda b, c: (b, 0, 0, 0))
    return pl.pallas_call(
        kern,
        grid=(nb, nc),
        in_specs=[row(CONV_DIM, COL_XBC // CONV_DIM), row(SSM_D_INNER, COL_Z // SSM_D_INNER),
                  row(LANES, COL_DT // LANES),
                  pl.BlockSpec((1, CONV_W - 1, CONV_DIM), lambda b, c: (b, 0, 0)), state_spec,
                  const(CONV_W, CONV_DIM), const(1, CONV_DIM), const(1, LANES), const(1, LANES),
                  const(1, SSM_D_INNER), const(1, SSM_D_INNER)],
        out_specs=[pl.BlockSpec((q, SSM_D_INNER), lambda b, c: (b * nc + c, 0)), state_spec],
        out_shape=[jax.ShapeDtypeStruct((nb * seq, SSM_D_INNER), out_dtype),
                   jax.ShapeDtypeStruct((nb, SSM_GROUPS, GROUP_P, SSM_D_STATE), f32)],
        scratch_shapes=[pltpu.VMEM((q + CONV_HALO, CONV_DIM), f32),
                        pltpu.VMEM((SSM_GROUPS, GROUP_P, SSM_D_STATE), f32),
                        pltpu.VMEM((q, SSM_D_INNER), f32)],
        compiler_params=_cparams(("parallel", "arbitrary"), 40),
        name="ssd_mixer",
    )(proj, proj, proj, hist, h0, conv_w.astype(f32), conv_b.reshape(1, CONV_DIM).astype(f32),
      pad_l(dt_bias), pad_l(a), jnp.repeat(d_skip.astype(f32), SSM_HEAD_DIM).reshape(1, SSM_D_INNER),
      norm_g.reshape(1, SSM_D_INNER).astype(f32))


def _pool_mix_kernel(cur_ref, halo_ref, o_ref, ext_sc, *, tm, pos0, zero_first_halo):
    i = pl.program_id(1)
    if zero_first_halo:
        ext_sc[0:HALO, :] = jnp.where(i == 0, 0.0, halo_ref[...])
    else:
        ext_sc[HALO - POOL_HIST:HALO, :] = halo_ref[0]
    ext_sc[HALO:HALO + tm, :] = cur_ref[...]
    gd = cur_ref.shape[1] // len(POOL_WINDOWS)
    count_pos = (pos0 + i * tm + 1 + lax.broadcasted_iota(jnp.int32, (tm, 1), 0)).astype(f32)
    for gi, win in enumerate(POOL_WINDOWS):
        cols = slice(gi * gd, (gi + 1) * gd)
        u = ext_sc[HALO:HALO + tm, cols]
        wsum = u
        for k in range(1, win):
            wsum = wsum + ext_sc[pl.ds(HALO - k, tm), cols]
        mean = wsum / jnp.minimum(count_pos, float(win))
        o_ref[:, cols] = (mean - u).astype(o_ref.dtype)


def pool_mix(u, halo, *, nb, seq, tm, row0, pos0, halo_from_u):
    d = u.shape[1]
    nt = seq // tm
    blk0 = row0 // tm
    assert row0 % tm == 0 and (halo_from_u or nt == 1)
    if halo_from_u:
        assert tm % HALO == 0
        hb = tm // HALO
        halo_spec = pl.BlockSpec((HALO, d), lambda b, i: (jnp.maximum((blk0 + b * nt + i) * hb - 1, 0), 0))
        halo = u
    else:
        halo_spec = pl.BlockSpec((1, POOL_HIST, d), lambda b, i: (b, 0, 0))
    kern = functools.partial(_pool_mix_kernel, tm=tm, pos0=pos0, zero_first_halo=halo_from_u)
    return pl.pallas_call(
        kern,
        grid=(nb, nt),
        in_specs=[pl.BlockSpec((tm, d), lambda b, i: (blk0 + b * nt + i, 0)), halo_spec],
        out_specs=pl.BlockSpec((tm, d), lambda b, i: (b * nt + i, 0)),
        out_shape=jax.ShapeDtypeStruct((nb * seq, d), bf16),
        scratch_shapes=[pltpu.VMEM((HALO + tm, d), f32)],
        compiler_params=_cparams(("parallel", "arbitrary"), 40),
        name="pool_mix",
    )(u, halo)


def _rope_tables(pos):
    inv_freq = 1.0 / (ROPE_THETA ** (jnp.arange(0, QK_ROPE, 2, dtype=f32) / QK_ROPE))
    ang = pos.astype(f32)[:, None] * inv_freq[None, :]
    cos, sin = jnp.cos(ang), jnp.sin(ang)
    zeros = jnp.zeros((pos.shape[0], ROPE_PAD - QK_ROPE), f32)
    return (jnp.concatenate([cos, cos, zeros], axis=1), jnp.concatenate([-sin, sin, zeros], axis=1))


def _swap_halves(w):
    half = w.shape[-1] // 2
    return jnp.concatenate([w[..., half:], w[..., :half]], axis=-1)


def _pad_last(w, n):
    return jnp.pad(w, [(0, 0)] * (w.ndim - 1) + [(0, n - w.shape[-1])])


def kernel(x_prompt, x_sample, cache_mla_latent, cache_mla_rope, state_ssm_conv, state_ssm, state_pool, page_table, norm_mix_g, norm_ffn_g, norm_final_g, w_in, q_norm_g, kv_norm_g, w_uq, w_ukv, conv_w, conv_b, dt_bias, a_log, d_skip, ssm_norm_g, w_out, w_pool, pool_scale, w_gate, w_up, w_down):
    bp, lp, d_model = x_prompt.shape
    bs, ls, _ = x_sample.shape
    mp, ms_rows = bp * lp, bs * ls
    li = 0

    wi = w_in[li]
    o_kv = Q_LORA
    o_kr = o_kv + KV_LORA
    o_z = o_kr + QK_ROPE
    o_xbc = o_z + SSM_D_INNER
    o_dt = o_xbc + CONV_DIM
    w_kr = wi[:, o_kr:o_z]
    w_in_r = jnp.concatenate(
        [wi[:, o_xbc:o_dt], wi[:, o_z:o_xbc], wi[:, :o_kv], wi[:, o_kv:o_kr],
         _pad_last(w_kr, ROPE_PAD), _pad_last(_swap_halves(w_kr), ROPE_PAD),
         _pad_last(wi[:, o_dt:], 2 * LANES)], axis=1).astype(bf16)
    assert w_in_r.shape[1] == PROJ_W
    wq = w_uq[li]
    wq_nope = wq[:, :, :QK_NOPE].transpose(1, 0, 2).astype(bf16)
    wq_rope = wq[:, :, QK_NOPE:].transpose(1, 0, 2)
    wq_r = _pad_last(wq_rope, ROPE_PAD).astype(bf16)
    wq_rr = _pad_last(_swap_halves(wq_rope), ROPE_PAD).astype(bf16)
    wkv = w_ukv[li]
    w_uk = wkv[:, :, :QK_NOPE].transpose(1, 2, 0).astype(bf16)
    w_uv = wkv[:, :, QK_NOPE:].transpose(1, 0, 2).astype(bf16)
    w_out_b = w_out[li].astype(bf16)
    w_pool_b = w_pool[0].astype(bf16)
    w_down_b = w_down.astype(bf16)

    pos = jnp.concatenate([jnp.tile(jnp.arange(lp), bp), jnp.tile(PAST_LEN + jnp.arange(ls), bs)])
    cos_t, sin_t = _rope_tables(pos)

    xp2 = x_prompt.reshape(mp, d_model)
    xs2 = x_sample.reshape(ms_rows, d_model)

    u = rmsnorm_stacked(xp2, xs2, norm_mix_g[0], tm=256, out_dtype=bf16)
    proj = matmul(u, w_in_r, tm=1024, tn=256, out_dtype=f32, name="in_proj")
    ckv, kpe, kcat = kv_prep(proj, cos_t, sin_t, kv_norm_g[li], tm=512)

    q_w = (q_norm_g[li], wq_nope, wq_r, wq_rr, w_uk)
    q_p = q_prep(proj, cos_t, sin_t, *q_w, rows=mp, row_blk0=0, tm=1024, out_dtype=bf16)
    q_s = q_prep(proj, cos_t, sin_t, *q_w, rows=ms_rows, row_blk0=mp // 1024, tm=1024, out_dtype=f32)
    o_p = attn_prompt(q_p, kcat[:mp].reshape(bp, lp, QK_CAT), w_uv, nb=bp, seq=lp, tq=128, tk=512, row_chunks=4)
    o_s = attn_sample(page_table, q_s, ckv, kpe, cache_mla_latent, jnp.swapaxes(cache_mla_rope, 2, 3), w_uv,
                      li=li, row0=mp, t_new=ls, ch=16, n_sub=4, n_slots=3)

    hist_p = jnp.zeros((bp, CONV_W - 1, CONV_DIM), f32)
    h0_p = jnp.zeros((bp, SSM_GROUPS, GROUP_P, SSM_D_STATE), f32)
    h0_s = state_ssm[li].reshape(bs, SSM_GROUPS, GROUP_P, SSM_D_STATE)
    ssd_w = (conv_w[li], conv_b[li], dt_bias[li], a_log[li], d_skip[li], ssm_norm_g[li])
    y_p, h_p = ssd_mixer(proj, hist_p, h0_p, *ssd_w, nb=bp, seq=lp, q=SSD_CHUNK, row0=0,
                         mxu_dtype=bf16, out_dtype=bf16)
    y_s, h_s = ssd_mixer(proj, state_ssm_conv[li], h0_s, *ssd_w, nb=bs, seq=ls, q=ls, row0=mp,
                         mxu_dtype=f32, out_dtype=f32)

    x1 = out_proj(o_p.reshape(mp, -1), y_p, o_s.reshape(ms_rows, -1).astype(bf16), y_s.astype(bf16), w_out_b,
                  xp2, xs2, tm=1024, tn=256)

    def ffn(x, layer):
        uf = rmsnorm(x, norm_ffn_g[layer], tm=256, out_dtype=bf16)
        hid = ffn_up(uf, w_gate, w_up, layer, tm=1024, tn=256)
        return ffn_down(hid, w_down_b, layer, x, tm=512, tn=256)

    x2 = ffn(x1, 0)

    u1 = rmsnorm(x2, norm_mix_g[1], tm=256, out_dtype=f32)
    mix_p = pool_mix(u1, None, nb=bp, seq=lp, tm=256, row0=0, pos0=0, halo_from_u=True)
    mix_s = pool_mix(u1, state_pool[0], nb=bs, seq=ls, tm=ls, row0=mp, pos0=PAST_LEN, halo_from_u=False)
    x3 = pool_matmul(mix_p, mix_s, w_pool_b, pool_scale[0], x2, tm=1024, tn=512)
    x4 = ffn(x3, 1)

    y_prompt = rmsnorm(x4, norm_final_g, tm=256, out_dtype=f32, rows=mp).reshape(bp, lp, d_model)
    y_sample = rmsnorm(x4, norm_final_g, tm=256, out_dtype=f32, rows=ms_rows,
                       row_blk0=mp // 256).reshape(bs, ls, d_model)

    tail = lambda v, n, width: jnp.stack([v[(b + 1) * lp - n:(b + 1) * lp, :width] for b in range(bp)])
    u1_s = u1[mp:].reshape(bs, ls, d_model)
    new_pool_s = jnp.concatenate([state_pool[0], u1_s], axis=1)[:, -POOL_HIST:]
    new_conv_s = jnp.concatenate([state_ssm_conv[li], proj[mp:, :CONV_DIM].reshape(bs, ls, CONV_DIM)],
                                 axis=1)[:, -(CONV_W - 1):]
    st = lambda h, nb: h.reshape(nb, SSM_HEADS, SSM_HEAD_DIM, SSM_D_STATE)
    return (y_prompt, y_sample,
            ckv[:mp].reshape(1, bp, lp, KV_LORA), kpe[:mp, :QK_ROPE].reshape(1, bp, lp, QK_ROPE),
            ckv[mp:].reshape(1, bs, ls, KV_LORA), kpe[mp:, :QK_ROPE].reshape(1, bs, ls, QK_ROPE),
            tail(proj, CONV_W - 1, CONV_DIM)[None], new_conv_s[None], st(h_p, bp)[None], st(h_s, bs)[None],
            tail(u1, POOL_HIST, d_model)[None], new_pool_s[None])
```

```python
import functools

import jax
import jax.numpy as jnp
from jax import lax
from jax.experimental import pallas as pl
from jax.experimental.pallas import tpu as pltpu

f32 = jnp.float32
bf16 = jnp.bfloat16

RMS_EPS = 1e-6
ROPE_THETA = 10000.0
PAST_LEN = 16384
PAGE_SIZE = 128
MLA_HEADS = 16
QK_NOPE = 128
QK_ROPE = 64
V_DIM = 128
Q_LORA = 1024
KV_LORA = 512
ATTN_SCALE = (QK_NOPE + QK_ROPE) ** -0.5
SSM_D_INNER = 2048
SSM_HEAD_DIM = 64
SSM_HEADS = 32
SSM_GROUPS = 8
SSM_D_STATE = 128
CONV_W = 4
CONV_DIM = SSM_D_INNER + 2 * SSM_GROUPS * SSM_D_STATE
SSD_CHUNK = 128
POOL_WINDOWS = (2, 4, 8, 16)
POOL_HIST = 15

LANES = 128
SUBLANES = 8
GROUP_P = SSM_D_INNER // SSM_GROUPS
HEADS_PER_GROUP = SSM_HEADS // SSM_GROUPS
ROPE_PAD = LANES
QK_CAT = KV_LORA + ROPE_PAD
HALO = 16
CONV_HALO = SUBLANES

COL_XBC = 0
COL_Z = CONV_DIM
COL_Q = COL_Z + SSM_D_INNER
COL_KV = COL_Q + Q_LORA
COL_KR = COL_KV + KV_LORA
COL_KRR = COL_KR + ROPE_PAD
COL_DT = COL_KRR + ROPE_PAD
PROJ_W = COL_DT + 2 * LANES


def _dot(a, b):
    return jnp.dot(a, b, preferred_element_type=f32)


def _dot_nt(a, b):
    return lax.dot_general(a, b, (((1,), (1,)), ((), ())), preferred_element_type=f32)


def _sigmoid(x):
    return 1.0 / (1.0 + jnp.exp(-x))


def _split3(x):
    hi = x.astype(bf16).astype(f32)
    r1 = x - hi
    mid = r1.astype(bf16).astype(f32)
    lo = (r1 - mid).astype(bf16).astype(f32)
    return hi, mid, lo


def _cparams(sem, vmem_mib):
    return pltpu.CompilerParams(dimension_semantics=sem, vmem_limit_bytes=vmem_mib << 20)


def _norm_kernel(x_ref, g_ref, o_ref):
    x = x_ref[...].astype(f32)
    ms = jnp.mean(x * x, axis=-1, keepdims=True)
    o_ref[...] = (x * lax.rsqrt(ms + RMS_EPS) * g_ref[...]).astype(o_ref.dtype)


def rmsnorm(x, g, *, tm, out_dtype, rows=None, row_blk0=0):
    rows = x.shape[0] if rows is None else rows
    width = x.shape[1]
    return pl.pallas_call(
        _norm_kernel,
        grid=(rows // tm,),
        in_specs=[pl.BlockSpec((tm, width), lambda i: (i + row_blk0, 0)),
                  pl.BlockSpec((1, width), lambda i: (0, 0))],
        out_specs=pl.BlockSpec((tm, width), lambda i: (i, 0)),
        out_shape=jax.ShapeDtypeStruct((rows, width), out_dtype),
        compiler_params=_cparams(("parallel",), 40),
        name="rmsnorm",
    )(x, g.reshape(1, width).astype(f32))


def _two_group_specs(n_first, n_second, shape):
    return (pl.BlockSpec(shape, lambda i, *_: (jnp.minimum(i, n_first - 1), 0)),
            pl.BlockSpec(shape, lambda i, *_: (jnp.clip(i - n_first, 0, n_second - 1), 0)))


def _norm2_kernel(xp_ref, xs_ref, g_ref, o_ref, *, n_first):
    i = pl.program_id(0)

    @pl.when(i < n_first)
    def _():
        _norm_kernel(xp_ref, g_ref, o_ref)

    @pl.when(i >= n_first)
    def _():
        _norm_kernel(xs_ref, g_ref, o_ref)


def rmsnorm_stacked(xp, xs, g, *, tm, out_dtype):
    (mp, width), ms = xp.shape, xs.shape[0]
    n_first, n_second = mp // tm, ms // tm
    return pl.pallas_call(
        functools.partial(_norm2_kernel, n_first=n_first),
        grid=(n_first + n_second,),
        in_specs=[*_two_group_specs(n_first, n_second, (tm, width)),
                  pl.BlockSpec((1, width), lambda i: (0, 0))],
        out_specs=pl.BlockSpec((tm, width), lambda i: (i, 0)),
        out_shape=jax.ShapeDtypeStruct((mp + ms, width), out_dtype),
        compiler_params=_cparams(("arbitrary",), 40),
        name="rmsnorm_stacked",
    )(xp, xs, g.reshape(1, width).astype(f32))


def _mm_kernel(a_ref, w_ref, o_ref):
    o_ref[...] = _dot(a_ref[...], w_ref[...]).astype(o_ref.dtype)


def matmul(a, w, *, tm, tn, out_dtype, name="matmul"):
    m, k = a.shape
    n = w.shape[1]
    return pl.pallas_call(
        _mm_kernel,
        grid=(m // tm, n // tn),
        in_specs=[pl.BlockSpec((tm, k), lambda i, j: (i, 0)),
                  pl.BlockSpec((k, tn), lambda i, j: (0, j))],
        out_specs=pl.BlockSpec((tm, tn), lambda i, j: (i, j)),
        out_shape=jax.ShapeDtypeStruct((m, n), out_dtype),
        compiler_params=_cparams(("parallel", "arbitrary"), 56),
        name=name,
    )(a, w)


def _out_proj_kernel(op_ref, os_ref, yp_ref, ys_ref, w_ref, rp_ref, rs_ref, o_ref, *, n_first):
    i = pl.program_id(0)
    k1 = op_ref.shape[1]

    def project(a1_ref, a2_ref, r_ref):
        acc = _dot(a1_ref[...].astype(bf16), w_ref[:k1, :]) + _dot(a2_ref[...].astype(bf16), w_ref[k1:, :])
        o_ref[...] = acc + r_ref[...]

    @pl.when(i < n_first)
    def _():
        project(op_ref, yp_ref, rp_ref)

    @pl.when(i >= n_first)
    def _():
        project(os_ref, ys_ref, rs_ref)


def out_proj(o_p, y_p, o_s, y_s, w, res_p, res_s, *, tm, tn):
    (mp, k1), k2, ms = o_p.shape, y_p.shape[1], o_s.shape[0]
    n = w.shape[1]
    n_first, n_second = mp // tm, ms // tm
    res_p_spec = pl.BlockSpec((tm, tn), lambda i, j: (jnp.minimum(i, n_first - 1), j))
    res_s_spec = pl.BlockSpec((tm, tn), lambda i, j: (jnp.clip(i - n_first, 0, n_second - 1), j))
    return pl.pallas_call(
        functools.partial(_out_proj_kernel, n_first=n_first),
        grid=(n_first + n_second, n // tn),
        in_specs=[*_two_group_specs(n_first, n_second, (tm, k1)), *_two_group_specs(n_first, n_second, (tm, k2)),
                  pl.BlockSpec((k1 + k2, tn), lambda i, j: (0, j)), res_p_spec, res_s_spec],
        out_specs=pl.BlockSpec((tm, tn), lambda i, j: (i, j)),
        out_shape=jax.ShapeDtypeStruct((mp + ms, n), f32),
        compiler_params=_cparams(("arbitrary", "arbitrary"), 56),
        name="out_proj",
    )(o_p, o_s, y_p, y_s, w, res_p, res_s)


def _ffn_up_kernel(a_ref, wg_ref, wu_ref, o_ref):
    a = a_ref[...]
    g = _dot(a, wg_ref[0].astype(bf16))
    u = _dot(a, wu_ref[0].astype(bf16))
    o_ref[...] = (g * _sigmoid(g) * u).astype(o_ref.dtype)


def ffn_up(a, wg, wu, layer, *, tm, tn):
    m, k = a.shape
    n = wg.shape[2]
    wspec = pl.BlockSpec((1, k, tn), lambda i, j: (layer, 0, j))
    return pl.pallas_call(
        _ffn_up_kernel,
        grid=(m // tm, n // tn),
        in_specs=[pl.BlockSpec((tm, k), lambda i, j: (i, 0)), wspec, wspec],
        out_specs=pl.BlockSpec((tm, tn), lambda i, j: (i, j)),
        out_shape=jax.ShapeDtypeStruct((m, n), bf16),
        compiler_params=_cparams(("parallel", "arbitrary"), 56),
        name="ffn_up",
    )(a, wg, wu)


def _ffn_down_kernel(a_ref, w_ref, r_ref, o_ref):
    o_ref[...] = _dot(a_ref[...], w_ref[0]) + r_ref[...]


def ffn_down(a, w, layer, residual, *, tm, tn):
    m, k = a.shape
    n = w.shape[2]
    return pl.pallas_call(
        _ffn_down_kernel,
        grid=(m // tm, n // tn),
        in_specs=[pl.BlockSpec((tm, k), lambda i, j: (i, 0)),
                  pl.BlockSpec((1, k, tn), lambda i, j: (layer, 0, j)),
                  pl.BlockSpec((tm, tn), lambda i, j: (i, j))],
        out_specs=pl.BlockSpec((tm, tn), lambda i, j: (i, j)),
        out_shape=jax.ShapeDtypeStruct((m, n), f32),
        compiler_params=_cparams(("parallel", "arbitrary"), 56),
        name="ffn_down",
    )(a, w, residual)


def _pool_mm_kernel(ap_ref, as_ref, w_ref, s_ref, r_ref, o_ref, *, n_first):
    i = pl.program_id(0)

    def project(a_ref):
        o_ref[...] = _dot(a_ref[...], w_ref[0]) * s_ref[...] + r_ref[...]

    @pl.when(i < n_first)
    def _():
        project(ap_ref)

    @pl.when(i >= n_first)
    def _():
        project(as_ref)


def pool_matmul(mix_p, mix_s, w, scale, residual, *, tm, tn):
    (mp, d), ms = mix_p.shape, mix_s.shape[0]
    ng, gd, _ = w.shape
    nj = gd // tn
    n_first, n_second = mp // tm, ms // tm
    return pl.pallas_call(
        functools.partial(_pool_mm_kernel, n_first=n_first),
        grid=(n_first + n_second, ng, nj),
        in_specs=[pl.BlockSpec((tm, gd), lambda i, g, j: (jnp.minimum(i, n_first - 1), g)),
                  pl.BlockSpec((tm, gd), lambda i, g, j: (jnp.clip(i - n_first, 0, n_second - 1), g)),
                  pl.BlockSpec((1, gd, tn), lambda i, g, j: (g, 0, j)),
                  pl.BlockSpec((1, tn), lambda i, g, j: (0, g * nj + j)),
                  pl.BlockSpec((tm, tn), lambda i, g, j: (i, g * nj + j))],
        out_specs=pl.BlockSpec((tm, tn), lambda i, g, j: (i, g * nj + j)),
        out_shape=jax.ShapeDtypeStruct((mp + ms, d), f32),
        compiler_params=_cparams(("arbitrary", "arbitrary", "arbitrary"), 40),
        name="pool_matmul",
    )(mix_p, mix_s, w, scale.reshape(1, d).astype(f32), residual)


def _kv_prep_kernel(kv_ref, kr_ref, krr_ref, cos_ref, sin_ref, g_ref, ckv_ref, kpe_ref, kcat_ref):
    x = kv_ref[...]
    ms = jnp.mean(x * x, axis=-1, keepdims=True)
    ckv = x * lax.rsqrt(ms + RMS_EPS) * g_ref[...]
    kpe = kr_ref[...] * cos_ref[...] + krr_ref[...] * sin_ref[...]
    ckv_ref[...] = ckv
    kpe_ref[...] = kpe
    kcat_ref[:, :KV_LORA] = ckv.astype(bf16)
    kcat_ref[:, KV_LORA:] = kpe.astype(bf16)


def kv_prep(proj, cos_t, sin_t, kv_norm_g, *, tm):
    m = proj.shape[0]
    row = lambda w, cb: pl.BlockSpec((tm, w), lambda i: (i, cb))
    return pl.pallas_call(
        _kv_prep_kernel,
        grid=(m // tm,),
        in_specs=[row(KV_LORA, COL_KV // KV_LORA), row(ROPE_PAD, COL_KR // ROPE_PAD),
                  row(ROPE_PAD, COL_KRR // ROPE_PAD), row(ROPE_PAD, 0), row(ROPE_PAD, 0),
                  pl.BlockSpec((1, KV_LORA), lambda i: (0, 0))],
        out_specs=[row(KV_LORA, 0), row(ROPE_PAD, 0), row(QK_CAT, 0)],
        out_shape=[jax.ShapeDtypeStruct((m, KV_LORA), f32), jax.ShapeDtypeStruct((m, ROPE_PAD), f32),
                   jax.ShapeDtypeStruct((m, QK_CAT), bf16)],
        compiler_params=_cparams(("parallel",), 32),
        name="kv_prep",
    )(proj, proj, proj, cos_t, sin_t, kv_norm_g.reshape(1, KV_LORA).astype(f32))


def _q_prep_kernel(cq_ref, g_ref, wn_ref, wr_ref, wrr_ref, wuk_ref, cos_ref, sin_ref, q_ref, cqn_sc):
    @pl.when(pl.program_id(1) == 0)
    def _():
        x = cq_ref[...]
        ms = jnp.mean(x * x, axis=-1, keepdims=True)
        cqn_sc[...] = (x * lax.rsqrt(ms + RMS_EPS) * g_ref[...]).astype(bf16)

    cqn = cqn_sc[...]
    q_nope = _dot(cqn, wn_ref[0]).astype(bf16)
    q_r = _dot(cqn, wr_ref[0])
    q_rr = _dot(cqn, wrr_ref[0])
    for e in range(wuk_ref.shape[0]):
        q_lat = _dot(q_nope[:, e * QK_NOPE:(e + 1) * QK_NOPE], wuk_ref[e])
        cols = slice(e * ROPE_PAD, (e + 1) * ROPE_PAD)
        q_pe = q_r[:, cols] * cos_ref[...] + q_rr[:, cols] * sin_ref[...]
        q_ref[e, :, :KV_LORA] = (q_lat * ATTN_SCALE).astype(q_ref.dtype)
        q_ref[e, :, KV_LORA:] = (q_pe * ATTN_SCALE).astype(q_ref.dtype)


def _side_by_side(w, hps):
    nh, k, n = w.shape
    return w.reshape(nh // hps, hps, k, n).transpose(0, 2, 1, 3).reshape(nh // hps, k, hps * n)


def q_prep(proj, cos_t, sin_t, q_norm_g, wn, wr, wrr, wuk, *, rows, row_blk0, tm, out_dtype, hps=2):
    nh = wn.shape[0]
    hw = lambda a, b: pl.BlockSpec((1, a, hps * b), lambda i, h: (h, 0, 0))
    return pl.pallas_call(
        _q_prep_kernel,
        grid=(rows // tm, nh // hps),
        in_specs=[pl.BlockSpec((tm, Q_LORA), lambda i, h: (i + row_blk0, COL_Q // Q_LORA)),
                  pl.BlockSpec((1, Q_LORA), lambda i, h: (0, 0)),
                  hw(Q_LORA, QK_NOPE), hw(Q_LORA, ROPE_PAD), hw(Q_LORA, ROPE_PAD),
                  pl.BlockSpec((hps, QK_NOPE, KV_LORA), lambda i, h: (h, 0, 0)),
                  pl.BlockSpec((tm, ROPE_PAD), lambda i, h: (i + row_blk0, 0)),
                  pl.BlockSpec((tm, ROPE_PAD), lambda i, h: (i + row_blk0, 0))],
        out_specs=pl.BlockSpec((hps, tm, QK_CAT), lambda i, h: (h, i, 0)),
        out_shape=jax.ShapeDtypeStruct((nh, rows, QK_CAT), out_dtype),
        scratch_shapes=[pltpu.VMEM((tm, Q_LORA), bf16)],
        compiler_params=_cparams(("parallel", "arbitrary"), 40),
        name="q_prep",
    )(proj, q_norm_g.reshape(1, Q_LORA).astype(f32), _side_by_side(wn, hps), _side_by_side(wr, hps),
      _side_by_side(wrr, hps), wuk, cos_t, sin_t)


def _attn_prompt_kernel(q_ref, k_ref, wuv_ref, o_ref, m_sc, l_sc, acc_sc, *, tq, tk, row_chunks):
    nh = q_ref.shape[0]
    hpc = nh // row_chunks
    rc = hpc * tq
    qb = pl.program_id(1)
    m_sc[...] = jnp.full(m_sc.shape, -1e30, f32)
    l_sc[...] = jnp.zeros(l_sc.shape, f32)
    acc_sc[...] = jnp.zeros(acc_sc.shape, f32)

    def step(kb, masked, width=tk):
        k = k_ref[0, pl.ds(pl.multiple_of(kb * tk, tk), width), :]
        v = k[:, :KV_LORA]
        for r in range(row_chunks):
            rs = slice(r * rc, (r + 1) * rc)
            q = q_ref[r * hpc:(r + 1) * hpc].reshape(rc, QK_CAT)
            s = _dot_nt(q, k)
            if masked:
                qpos = qb * tq + lax.broadcasted_iota(jnp.int32, (hpc, tq, width), 1).reshape(rc, width)
                kpos = kb * tk + lax.broadcasted_iota(jnp.int32, (rc, width), 1)
                s = jnp.where(kpos <= qpos, s, -jnp.inf)
            m_prev = m_sc[rs]
            m_new = jnp.maximum(m_prev, jnp.max(s, axis=-1, keepdims=True))
            alpha = jnp.exp(m_prev - m_new)
            p = jnp.exp(s - m_new)
            l_sc[rs] = alpha * l_sc[rs] + jnp.sum(p, axis=-1, keepdims=True)
            acc_sc[rs] = alpha * acc_sc[rs] + _dot(p.astype(bf16), v)
            m_sc[rs] = m_new

    n_full = (qb * tq) // tk

    def body(kb, carry):
        step(kb, False)
        return carry

    lax.fori_loop(0, n_full, body, 0)
    if tk % (2 * tq) == 0:
        first_half = (qb + 1) * tq - n_full * tk <= tk // 2

        @pl.when(first_half)
        def _():
            step(n_full, True, tk // 2)

        @pl.when(jnp.logical_not(first_half))
        def _():
            step(n_full, True)
    else:
        step(n_full, True)

    o = (acc_sc[...] / l_sc[...]).astype(bf16).reshape(nh, tq, KV_LORA)
    for h in range(nh):
        o_ref[0, :, h * V_DIM:(h + 1) * V_DIM] = _dot(o[h], wuv_ref[h]).astype(o_ref.dtype)


def attn_prompt(q, kcat, wuv, *, nb, seq, tq, tk, row_chunks):
    nh = q.shape[0]
    assert tk % tq == 0 and seq % tk == 0 and nh % row_chunks == 0
    nqb = seq // tq
    rows = nh * tq
    kern = functools.partial(_attn_prompt_kernel, tq=tq, tk=tk, row_chunks=row_chunks)
    return pl.pallas_call(
        kern,
        grid=(nb, nqb),
        in_specs=[pl.BlockSpec((nh, tq, QK_CAT), lambda b, i: (0, b * nqb + i, 0)),
                  pl.BlockSpec((1, seq, QK_CAT), lambda b, i: (b, 0, 0)),
                  pl.BlockSpec((nh, KV_LORA, V_DIM), lambda b, i: (0, 0, 0))],
        out_specs=pl.BlockSpec((1, tq, nh * V_DIM), lambda b, i: (b, i, 0)),
        out_shape=jax.ShapeDtypeStruct((nb, seq, nh * V_DIM), bf16),
        scratch_shapes=[pltpu.VMEM((rows, 1), f32), pltpu.VMEM((rows, 1), f32),
                        pltpu.VMEM((rows, KV_LORA), f32)],
        compiler_params=_cparams(("parallel", "arbitrary"), 48),
        name="attn_prompt",
    )(q, kcat, wuv)


def _attn_sample_kernel(pt_ref, q_ref, slat_ref, spe_ref, wuv_ref, clat_hbm, crope_hbm, o_ref,
                        lat_buf, rope_buf, sem, m_sc, l_sc, acc_sc, q_sc, *, li, ch, n_sub, t_new):
    nh = q_ref.shape[0]
    rows = nh * t_new
    n_chunks = pt_ref.shape[1] // ch
    ps = ch // n_sub
    b = pl.program_id(0)
    nb = pl.num_programs(0)

    def page_copies(pid, slot, j):
        return (pltpu.make_async_copy(clat_hbm.at[li, pid], lat_buf.at[slot, j], sem.at[0, slot]),
                pltpu.make_async_copy(crope_hbm.at[li, pid], rope_buf.at[slot, j], sem.at[1, slot]))

    def start_chunk(bb, c, slot):
        for j in range(ch):
            for cp in page_copies(pt_ref[bb, c * ch + j], slot, j):
                cp.start()

    def wait_chunk(slot):
        for j in range(ch):
            for cp in page_copies(0, slot, j):
                cp.wait()

    n_slots = lat_buf.shape[0]
    ahead = n_slots - 1

    def chunk_at(g):
        return b + g // n_chunks, g % n_chunks

    @pl.when(b == 0)
    def _():
        for g in range(ahead):
            start_chunk(g // n_chunks, g % n_chunks, g % n_slots)

    q = q_ref[...].reshape(rows, QK_CAT).astype(bf16)
    q_sc[...] = q
    pad = 2 * SUBLANES - t_new
    ks = jnp.concatenate([jnp.concatenate([slat_ref[...], spe_ref[...]], axis=1),
                          jnp.zeros((pad, QK_CAT), f32)], axis=0).astype(bf16)
    s = _dot_nt(q, ks)
    tpos = lax.broadcasted_iota(jnp.int32, (nh, t_new, 2 * SUBLANES), 1).reshape(rows, 2 * SUBLANES)
    kpos = lax.broadcasted_iota(jnp.int32, (rows, 2 * SUBLANES), 1)
    s = jnp.where(kpos <= tpos, s, -jnp.inf)
    m0 = jnp.maximum(jnp.max(s, axis=-1, keepdims=True), -1e30)
    p = jnp.exp(s - m0)
    m_sc[...] = m0
    l_sc[...] = jnp.sum(p, axis=-1, keepdims=True)
    acc_sc[...] = _dot(p.astype(bf16), ks[:, :KV_LORA])

    def softmax_part(slot, sub):
        k = lat_buf[slot, sub * ps:(sub + 1) * ps].reshape(ps * PAGE_SIZE, KV_LORA).astype(bf16)
        kr_t = jnp.concatenate([rope_buf[slot, sub * ps + j] for j in range(ps)], axis=1).astype(bf16)
        s = _dot_nt(q_sc[:, :KV_LORA], k) + _dot(q_sc[:, KV_LORA:KV_LORA + QK_ROPE], kr_t)
        m_s = jnp.max(s, axis=-1, keepdims=True)
        p = jnp.exp(s - m_s)
        return m_s, jnp.sum(p, axis=-1, keepdims=True), _dot(p.astype(bf16), k)

    def merge(parts):
        m_old = m_sc[...]
        m_new = m_old
        for m_s, _, _ in parts:
            m_new = jnp.maximum(m_new, m_s)
        alpha = jnp.exp(m_old - m_new)
        l = alpha * l_sc[...]
        acc = alpha * acc_sc[...]
        for m_s, l_s, o_s in parts:
            w = jnp.exp(m_s - m_new)
            l = l + w * l_s
            acc = acc + w * o_s
        m_sc[...] = m_new
        l_sc[...] = l
        acc_sc[...] = acc

    def chunk(c, carry):
        g = b * n_chunks + c
        slot = g % n_slots
        wait_chunk(slot)
        bb, cc = chunk_at(c + ahead)

        @pl.when(bb < nb)
        def _():
            start_chunk(bb, cc, (g + ahead) % n_slots)

        merge([softmax_part(slot, sub) for sub in range(n_sub)])
        return carry

    lax.fori_loop(0, n_chunks, chunk, 0)

    o = (acc_sc[...] / l_sc[...]).astype(bf16).astype(f32).reshape(nh, t_new, KV_LORA)
    for h in range(nh):
        o_ref[0, :, h * V_DIM:(h + 1) * V_DIM] = _dot(o[h], wuv_ref[h].astype(f32))


def attn_sample(page_table, q, ckv, kpe, cache_lat, cache_rope_t, wuv, *, li, row0, t_new, ch, n_sub, n_slots):
    nh = q.shape[0]
    nb, n_pages = page_table.shape
    assert n_pages % ch == 0 and ch % n_sub == 0 and row0 % t_new == 0
    assert 2 <= n_slots <= nb * (n_pages // ch)
    rows = nh * t_new
    blk0 = row0 // t_new
    kern = functools.partial(_attn_sample_kernel, li=li, ch=ch, n_sub=n_sub, t_new=t_new)
    grid_spec = pltpu.PrefetchScalarGridSpec(
        num_scalar_prefetch=1,
        grid=(nb,),
        in_specs=[pl.BlockSpec((nh, t_new, QK_CAT), lambda b, pt: (0, b, 0)),
                  pl.BlockSpec((t_new, KV_LORA), lambda b, pt: (blk0 + b, 0)),
                  pl.BlockSpec((t_new, ROPE_PAD), lambda b, pt: (blk0 + b, 0)),
                  pl.BlockSpec((nh, KV_LORA, V_DIM), lambda b, pt: (0, 0, 0)),
                  pl.BlockSpec(memory_space=pl.ANY),
                  pl.BlockSpec(memory_space=pl.ANY)],
        out_specs=pl.BlockSpec((1, t_new, nh * V_DIM), lambda b, pt: (b, 0, 0)),
        scratch_shapes=[pltpu.VMEM((n_slots, ch, PAGE_SIZE, KV_LORA), f32),
                        pltpu.VMEM((n_slots, ch, QK_ROPE, PAGE_SIZE), f32),
                        pltpu.SemaphoreType.DMA((2, n_slots)),
                        pltpu.VMEM((rows, 1), f32), pltpu.VMEM((rows, 1), f32),
                        pltpu.VMEM((rows, KV_LORA), f32),
                        pltpu.VMEM((rows, QK_CAT), bf16)],
    )
    return pl.pallas_call(
        kern,
        grid_spec=grid_spec,
        out_shape=jax.ShapeDtypeStruct((nb, t_new, nh * V_DIM), f32),
        compiler_params=_cparams(("arbitrary",), 40),
        name="attn_sample",
    )(page_table, q, ckv, kpe, wuv, cache_lat, cache_rope_t)


def _expand_heads(cols, n):
    q = cols.shape[0]
    lane_head = lax.broadcasted_iota(jnp.int32, (q, GROUP_P), 1) // SSM_HEAD_DIM
    out = jnp.broadcast_to(cols[:, 0:1], (q, GROUP_P))
    for r in range(1, n):
        out = jnp.where(lane_head == r, cols[:, r:r + 1], out)
    return out


def _ssd_kernel(xbc_ref, z_ref, dt_ref, hist_ref, h0_ref, cw_ref, cb_ref, dtb_ref, a_ref, dsk_ref, ng_ref,
                y_ref, hout_ref, ext_sc, h_sc, y_sc, *, q, seqs_per_step, single_chunk, mxu_dtype):
    for bi in range(seqs_per_step):
        rows = pl.ds(bi * q, q)
        _ssd_sequence(xbc_ref.at[rows], z_ref.at[rows], dt_ref.at[rows], hist_ref.at[bi], h0_ref.at[bi],
                      cw_ref, cb_ref, dtb_ref, a_ref, dsk_ref, ng_ref, y_ref.at[rows], hout_ref.at[bi],
                      ext_sc.at[bi], h_sc.at[bi], y_sc.at[bi], q=q, single_chunk=single_chunk,
                      mxu_dtype=mxu_dtype)


def _ssd_sequence(xbc_ref, z_ref, dt_ref, hist_ref, h0_ref, cw_ref, cb_ref, dtb_ref, a_ref, dsk_ref, ng_ref,
                  y_ref, hout_ref, ext_sc, h_sc, y_sc, *, q, single_chunk, mxu_dtype):
    c = pl.program_id(1)
    cast = lambda v: v.astype(mxu_dtype)
    when = (lambda cond: (lambda fn: fn())) if single_chunk else pl.when

    @when(c == 0)
    def _():
        ext_sc[CONV_HALO - (CONV_W - 1):CONV_HALO, :] = hist_ref[...]
        h_sc[...] = h0_ref[...]

    ext_sc[CONV_HALO:CONV_HALO + q, :] = xbc_ref[...]
    conv = cb_ref[...]
    for k in range(CONV_W):
        conv = conv + ext_sc[pl.ds(CONV_HALO - (CONV_W - 1) + k, q), :] * cw_ref[k:k + 1, :]
    ext_sc[0:CONV_HALO, :] = ext_sc[q:q + CONV_HALO, :]
    act = conv * _sigmoid(conv)

    dt_in = dt_ref[...] + dtb_ref[...]
    dt = jnp.maximum(dt_in, 0.0) + jnp.log1p(jnp.exp(-jnp.abs(dt_in)))
    dta = dt * a_ref[...]
    ri = lax.broadcasted_iota(jnp.int32, (q, q), 0)
    ci = lax.broadcasted_iota(jnp.int32, (q, q), 1)
    tril = ci <= ri
    trif = tril.astype(f32)
    eye_l = (lax.broadcasted_iota(jnp.int32, (LANES, LANES), 0)
             == lax.broadcasted_iota(jnp.int32, (LANES, LANES), 1)).astype(f32)
    eye_p = (lax.broadcasted_iota(jnp.int32, (GROUP_P, GROUP_P), 0)
             == lax.broadcasted_iota(jnp.int32, (GROUP_P, GROUP_P), 1)).astype(mxu_dtype)
    a_cum = sum(_dot(trif, part) for part in _split3(dta))
    a_cum_t = sum(_dot_nt(eye_l, part) for part in _split3(a_cum))
    dt_t = sum(_dot_nt(eye_l, part) for part in _split3(dt))
    a_last = a_cum[q - 1:q, :]
    decay = jnp.exp(a_last - a_cum) * dt
    e_acum = jnp.exp(a_cum)
    e_last_t = jnp.exp(a_cum_t[:, q - 1:q])
    lane_head = lax.broadcasted_iota(jnp.int32, (q, GROUP_P), 1) // SSM_HEAD_DIM

    for g in range(SSM_GROUPS):
        h_lo = g * HEADS_PER_GROUP
        bg = cast(act[:, SSM_D_INNER + g * SSM_D_STATE:SSM_D_INNER + (g + 1) * SSM_D_STATE])
        cg = cast(act[:, SSM_D_INNER + (SSM_GROUPS + g) * SSM_D_STATE:
                      SSM_D_INNER + (SSM_GROUPS + g + 1) * SSM_D_STATE])
        xg = act[:, g * GROUP_P:(g + 1) * GROUP_P]
        xg_m = cast(xg)
        cb = _dot_nt(cg, bg)
        hg = h_sc[g]
        y_off = _dot_nt(cg, cast(hg))
        y_diag = jnp.zeros((q, GROUP_P), f32)
        for r in range(HEADS_PER_GROUP):
            h = h_lo + r
            seg = a_cum[:, h:h + 1] - a_cum_t[h:h + 1, :]
            w = cb * jnp.exp(jnp.where(tril, seg, -jnp.inf)) * dt_t[h:h + 1, :]
            y_diag = jnp.where(lane_head == r, _dot(cast(w), xg_m), y_diag)
        e_g = _expand_heads(e_acum[:, h_lo:h_lo + HEADS_PER_GROUP], HEADS_PER_GROUP)
        d_g = _expand_heads(decay[:, h_lo:h_lo + HEADS_PER_GROUP], HEADS_PER_GROUP)
        y_sc[:, g * GROUP_P:(g + 1) * GROUP_P] = (
            y_diag + y_off * e_g + dsk_ref[:, g * GROUP_P:(g + 1) * GROUP_P] * xg)
        xdec_t = cast(_dot_nt(eye_p, cast(xg * d_g)))
        states = _dot(xdec_t, bg)
        cdec = jnp.concatenate(
            [jnp.broadcast_to(e_last_t[h_lo + r:h_lo + r + 1, :], (SSM_HEAD_DIM, SSM_D_STATE))
             for r in range(HEADS_PER_GROUP)], axis=0)
        h_sc[g] = hg * cdec + states

    gated = y_sc[...] * (z_ref[...] * _sigmoid(z_ref[...]))
    ms = jnp.mean(gated * gated, axis=-1, keepdims=True)
    y_ref[...] = (gated * lax.rsqrt(ms + RMS_EPS) * ng_ref[...]).astype(y_ref.dtype)

    @when(c == pl.num_programs(1) - 1)
    def _():
        hout_ref[...] = h_sc[...]


def ssd_mixer(proj, hist, h0, conv_w, conv_b, dt_bias, a_log, d_skip, norm_g, *, nb, seq, q, row0,
              mxu_dtype, out_dtype, seqs_per_step=1):
    sps = seqs_per_step
    nc = seq // q
    assert seq % q == 0 and row0 % (sps * q) == 0 and q >= CONV_HALO and nb % sps == 0
    assert sps == 1 or nc == 1
    blk0 = row0 // (sps * q)
    pad_l = lambda v: jnp.pad(v.astype(f32), (0, LANES - v.shape[0])).reshape(1, LANES)
    a = -jnp.exp(a_log.astype(f32))
    row = lambda w, cb: pl.BlockSpec((sps * q, w), lambda b, c: (blk0 + b * nc + c, cb))
    const = lambda r, w: pl.BlockSpec((r, w), lambda b, c: (0, 0))
    kern = functools.partial(_ssd_kernel, q=q, seqs_per_step=sps, single_chunk=nc == 1, mxu_dtype=mxu_dtype)
    state_spec = pl.BlockSpec((sps, SSM_GROUPS, GROUP_P, SSM_D_STATE), lambda b, c: (b, 0, 0, 0))
    return pl.pallas_call(
        kern,
        grid=(nb // sps, nc),
        in_specs=[row(CONV_DIM, COL_XBC // CONV_DIM), row(SSM_D_INNER, COL_Z // SSM_D_INNER),
                  row(LANES, COL_DT // LANES),
                  pl.BlockSpec((sps, CONV_W - 1, CONV_DIM), lambda b, c: (b, 0, 0)), state_spec,
                  const(CONV_W, CONV_DIM), const(1, CONV_DIM), const(1, LANES), const(1, LANES),
                  const(1, SSM_D_INNER), const(1, SSM_D_INNER)],
        out_specs=[pl.BlockSpec((sps * q, SSM_D_INNER), lambda b, c: (b * nc + c, 0)), state_spec],
        out_shape=[jax.ShapeDtypeStruct((nb * seq, SSM_D_INNER), out_dtype),
                   jax.ShapeDtypeStruct((nb, SSM_GROUPS, GROUP_P, SSM_D_STATE), f32)],
        scratch_shapes=[pltpu.VMEM((sps, q + CONV_HALO, CONV_DIM), f32),
                        pltpu.VMEM((sps, SSM_GROUPS, GROUP_P, SSM_D_STATE), f32),
                        pltpu.VMEM((sps, q, SSM_D_INNER), f32)],
        compiler_params=_cparams(("parallel", "arbitrary"), 40),
        name="ssd_mixer",
    )(proj, proj, proj, hist, h0, conv_w.astype(f32), conv_b.reshape(1, CONV_DIM).astype(f32),
      pad_l(dt_bias), pad_l(a), jnp.repeat(d_skip.astype(f32), SSM_HEAD_DIM).reshape(1, SSM_D_INNER),
      norm_g.reshape(1, SSM_D_INNER).astype(f32))


def _pool_mix_kernel(cur_ref, halo_ref, o_ref, ext_sc, *, tm, seqs_per_step, pos0, zero_first_halo):
    i = pl.program_id(1)
    gd = cur_ref.shape[1] // len(POOL_WINDOWS)
    count_pos = (pos0 + i * tm + 1 + lax.broadcasted_iota(jnp.int32, (tm, 1), 0)).astype(f32)
    for bi in range(seqs_per_step):
        rows = slice(bi * tm, (bi + 1) * tm)
        ext = ext_sc.at[bi]
        if zero_first_halo:
            ext[0:HALO, :] = jnp.where(i == 0, 0.0, halo_ref[...])
        else:
            ext[HALO - POOL_HIST:HALO, :] = halo_ref[bi]
        ext[HALO:HALO + tm, :] = cur_ref[rows, :]
        for gi, win in enumerate(POOL_WINDOWS):
            cols = slice(gi * gd, (gi + 1) * gd)
            u = ext[HALO:HALO + tm, cols]
            wsum = u
            for k in range(1, win):
                wsum = wsum + ext[pl.ds(HALO - k, tm), cols]
            mean = wsum / jnp.minimum(count_pos, float(win))
            o_ref[rows, cols] = (mean - u).astype(o_ref.dtype)


def pool_mix(u, halo, *, nb, seq, tm, row0, pos0, halo_from_u, seqs_per_step=1):
    d = u.shape[1]
    sps = seqs_per_step
    nt = seq // tm
    blk0 = row0 // (sps * tm)
    assert row0 % (sps * tm) == 0 and nb % sps == 0 and (halo_from_u or nt == 1) and (sps == 1 or nt == 1)
    if halo_from_u:
        assert tm % HALO == 0 and sps == 1
        hb = tm // HALO
        halo_spec = pl.BlockSpec((HALO, d), lambda b, i: (jnp.maximum((blk0 + b * nt + i) * hb - 1, 0), 0))
        halo = u
    else:
        halo_spec = pl.BlockSpec((sps, POOL_HIST, d), lambda b, i: (b, 0, 0))
    kern = functools.partial(_pool_mix_kernel, tm=tm, seqs_per_step=sps, pos0=pos0, zero_first_halo=halo_from_u)
    return pl.pallas_call(
        kern,
        grid=(nb // sps, nt),
        in_specs=[pl.BlockSpec((sps * tm, d), lambda b, i: (blk0 + b * nt + i, 0)), halo_spec],
        out_specs=pl.BlockSpec((sps * tm, d), lambda b, i: (b * nt + i, 0)),
        out_shape=jax.ShapeDtypeStruct((nb * seq, d), bf16),
        scratch_shapes=[pltpu.VMEM((sps, HALO + tm, d), f32)],
        compiler_params=_cparams(("parallel", "arbitrary"), 40),
        name="pool_mix",
    )(u, halo)


def _rope_tables(pos):
    inv_freq = 1.0 / (ROPE_THETA ** (jnp.arange(0, QK_ROPE, 2, dtype=f32) / QK_ROPE))
    ang = pos.astype(f32)[:, None] * inv_freq[None, :]
    cos, sin = jnp.cos(ang), jnp.sin(ang)
    zeros = jnp.zeros((pos.shape[0], ROPE_PAD - QK_ROPE), f32)
    return (jnp.concatenate([cos, cos, zeros], axis=1), jnp.concatenate([-sin, sin, zeros], axis=1))


def _swap_halves(w):
    half = w.shape[-1] // 2
    return jnp.concatenate([w[..., half:], w[..., :half]], axis=-1)


def _pad_last(w, n):
    return jnp.pad(w, [(0, 0)] * (w.ndim - 1) + [(0, n - w.shape[-1])])


def kernel(x_prompt, x_sample, cache_mla_latent, cache_mla_rope, state_ssm_conv, state_ssm, state_pool, page_table, norm_mix_g, norm_ffn_g, norm_final_g, w_in, q_norm_g, kv_norm_g, w_uq, w_ukv, conv_w, conv_b, dt_bias, a_log, d_skip, ssm_norm_g, w_out, w_pool, pool_scale, w_gate, w_up, w_down):
    bp, lp, d_model = x_prompt.shape
    bs, ls, _ = x_sample.shape
    mp, ms_rows = bp * lp, bs * ls
    li = 0

    wi = w_in[li]
    o_kv = Q_LORA
    o_kr = o_kv + KV_LORA
    o_z = o_kr + QK_ROPE
    o_xbc = o_z + SSM_D_INNER
    o_dt = o_xbc + CONV_DIM
    w_kr = wi[:, o_kr:o_z]
    w_in_r = jnp.concatenate(
        [wi[:, o_xbc:o_dt], wi[:, o_z:o_xbc], wi[:, :o_kv], wi[:, o_kv:o_kr],
         _pad_last(w_kr, ROPE_PAD), _pad_last(_swap_halves(w_kr), ROPE_PAD),
         _pad_last(wi[:, o_dt:], 2 * LANES)], axis=1).astype(bf16)
    assert w_in_r.shape[1] == PROJ_W
    wq = w_uq[li]
    wq_nope = wq[:, :, :QK_NOPE].transpose(1, 0, 2).astype(bf16)
    wq_rope = wq[:, :, QK_NOPE:].transpose(1, 0, 2)
    wq_r = _pad_last(wq_rope, ROPE_PAD).astype(bf16)
    wq_rr = _pad_last(_swap_halves(wq_rope), ROPE_PAD).astype(bf16)
    wkv = w_ukv[li]
    w_uk = wkv[:, :, :QK_NOPE].transpose(1, 2, 0).astype(bf16)
    w_uv = wkv[:, :, QK_NOPE:].transpose(1, 0, 2).astype(bf16)
    w_out_b = w_out[li].astype(bf16)
    w_pool_b = w_pool[0].astype(bf16)
    w_down_b = w_down.astype(bf16)

    pos = jnp.concatenate([jnp.tile(jnp.arange(lp), bp), jnp.tile(PAST_LEN + jnp.arange(ls), bs)])
    cos_t, sin_t = _rope_tables(pos)

    xp2 = x_prompt.reshape(mp, d_model)
    xs2 = x_sample.reshape(ms_rows, d_model)

    u = rmsnorm_stacked(xp2, xs2, norm_mix_g[0], tm=256, out_dtype=bf16)
    proj = matmul(u, w_in_r, tm=1024, tn=256, out_dtype=f32, name="in_proj")
    ckv, kpe, kcat = kv_prep(proj, cos_t, sin_t, kv_norm_g[li], tm=512)

    q_w = (q_norm_g[li], wq_nope, wq_r, wq_rr, w_uk)
    q_p = q_prep(proj, cos_t, sin_t, *q_w, rows=mp, row_blk0=0, tm=1024, out_dtype=bf16)
    q_s = q_prep(proj, cos_t, sin_t, *q_w, rows=ms_rows, row_blk0=mp // 1024, tm=1024, out_dtype=f32)
    o_p = attn_prompt(q_p, kcat[:mp].reshape(bp, lp, QK_CAT), w_uv, nb=bp, seq=lp, tq=256, tk=512, row_chunks=4)
    o_s = attn_sample(page_table, q_s, ckv, kpe, cache_mla_latent, jnp.swapaxes(cache_mla_rope, 2, 3), w_uv,
                      li=li, row0=mp, t_new=ls, ch=16, n_sub=4, n_slots=3)

    hist_p = jnp.zeros((bp, CONV_W - 1, CONV_DIM), f32)
    h0_p = jnp.zeros((bp, SSM_GROUPS, GROUP_P, SSM_D_STATE), f32)
    h0_s = state_ssm[li].reshape(bs, SSM_GROUPS, GROUP_P, SSM_D_STATE)
    ssd_w = (conv_w[li], conv_b[li], dt_bias[li], a_log[li], d_skip[li], ssm_norm_g[li])
    y_p, h_p = ssd_mixer(proj, hist_p, h0_p, *ssd_w, nb=bp, seq=lp, q=SSD_CHUNK, row0=0,
                         mxu_dtype=bf16, out_dtype=bf16)
    y_s, h_s = ssd_mixer(proj, state_ssm_conv[li], h0_s, *ssd_w, nb=bs, seq=ls, q=ls, row0=mp,
                         mxu_dtype=f32, out_dtype=f32, seqs_per_step=2)

    x1 = out_proj(o_p.reshape(mp, -1), y_p, o_s.reshape(ms_rows, -1).astype(bf16), y_s.astype(bf16), w_out_b,
                  xp2, xs2, tm=1024, tn=256)

    def ffn(x, layer):
        uf = rmsnorm(x, norm_ffn_g[layer], tm=256, out_dtype=bf16)
        hid = ffn_up(uf, w_gate, w_up, layer, tm=1536, tn=256)
        return ffn_down(hid, w_down_b, layer, x, tm=512, tn=512)

    x2 = ffn(x1, 0)

    u1 = rmsnorm(x2, norm_mix_g[1], tm=256, out_dtype=f32)
    mix_p = pool_mix(u1, None, nb=bp, seq=lp, tm=256, row0=0, pos0=0, halo_from_u=True)
    mix_s = pool_mix(u1, state_pool[0], nb=bs, seq=ls, tm=ls, row0=mp, pos0=PAST_LEN, halo_from_u=False,
                     seqs_per_step=8)
    x3 = pool_matmul(mix_p, mix_s, w_pool_b, pool_scale[0], x2, tm=1024, tn=512)
    x4 = ffn(x3, 1)

    y_prompt = rmsnorm(x4, norm_final_g, tm=256, out_dtype=f32, rows=mp).reshape(bp, lp, d_model)
    y_sample = rmsnorm(x4, norm_final_g, tm=256, out_dtype=f32, rows=ms_rows,
                       row_blk0=mp // 256).reshape(bs, ls, d_model)

    tail = lambda v, n, width: jnp.stack([v[(b + 1) * lp - n:(b + 1) * lp, :width] for b in range(bp)])
    u1_s = u1[mp:].reshape(bs, ls, d_model)
    new_pool_s = jnp.concatenate([state_pool[0], u1_s], axis=1)[:, -POOL_HIST:]
    new_conv_s = jnp.concatenate([state_ssm_conv[li], proj[mp:, :CONV_DIM].reshape(bs, ls, CONV_DIM)],
                                 axis=1)[:, -(CONV_W - 1):]
    st = lambda h, nb: h.reshape(nb, SSM_HEADS, SSM_HEAD_DIM, SSM_D_STATE)
    return (y_prompt, y_sample,
            ckv[:mp].reshape(1, bp, lp, KV_LORA), kpe[:mp, :QK_ROPE].reshape(1, bp, lp, QK_ROPE),
            ckv[mp:].reshape(1, bs, ls, KV_LORA), kpe[mp:, :QK_ROPE].reshape(1, bs, ls, QK_ROPE),
            tail(proj, CONV_W - 1, CONV_DIM)[None], new_conv_s[None], st(h_p, bp)[None], st(h_s, bs)[None],
            tail(u1, POOL_HIST, d_model)[None], new_pool_s[None])
```

```python
import functools

import jax
import jax.numpy as jnp
from jax import lax
from jax.experimental import pallas as pl
from jax.experimental.pallas import tpu as pltpu

f32 = jnp.float32
bf16 = jnp.bfloat16

RMS_EPS = 1e-6
ROPE_THETA = 10000.0
PAST_LEN = 16384
PAGE_SIZE = 128
MLA_HEADS = 16
QK_NOPE = 128
QK_ROPE = 64
V_DIM = 128
Q_LORA = 1024
KV_LORA = 512
ATTN_SCALE = (QK_NOPE + QK_ROPE) ** -0.5
SSM_D_INNER = 2048
SSM_HEAD_DIM = 64
SSM_HEADS = 32
SSM_GROUPS = 8
SSM_D_STATE = 128
CONV_W = 4
CONV_DIM = SSM_D_INNER + 2 * SSM_GROUPS * SSM_D_STATE
SSD_CHUNK = 128
POOL_WINDOWS = (2, 4, 8, 16)
POOL_HIST = 15

LANES = 128
SUBLANES = 8
GROUP_P = SSM_D_INNER // SSM_GROUPS
HEADS_PER_GROUP = SSM_HEADS // SSM_GROUPS
ROPE_PAD = LANES
QK_CAT = KV_LORA + ROPE_PAD
HALO = 16
CONV_HALO = SUBLANES

COL_XBC = 0
COL_Z = CONV_DIM
COL_Q = COL_Z + SSM_D_INNER
COL_KV = COL_Q + Q_LORA
COL_KR = COL_KV + KV_LORA
COL_KRR = COL_KR + ROPE_PAD
COL_DT = COL_KRR + ROPE_PAD
PROJ_W = COL_DT + 2 * LANES


def _dot(a, b):
    return jnp.dot(a, b, preferred_element_type=f32)


def _dot_nt(a, b):
    return lax.dot_general(a, b, (((1,), (1,)), ((), ())), preferred_element_type=f32)


def _sigmoid(x):
    return 1.0 / (1.0 + jnp.exp(-x))


def _split3(x):
    hi = x.astype(bf16).astype(f32)
    r1 = x - hi
    mid = r1.astype(bf16).astype(f32)
    lo = (r1 - mid).astype(bf16).astype(f32)
    return hi, mid, lo


def _cparams(sem, vmem_mib):
    return pltpu.CompilerParams(dimension_semantics=sem, vmem_limit_bytes=vmem_mib << 20)


def _norm_kernel(x_ref, g_ref, o_ref):
    x = x_ref[...].astype(f32)
    ms = jnp.mean(x * x, axis=-1, keepdims=True)
    o_ref[...] = (x * lax.rsqrt(ms + RMS_EPS) * g_ref[...]).astype(o_ref.dtype)


def rmsnorm(x, g, *, tm, out_dtype, rows=None, row_blk0=0):
    rows = x.shape[0] if rows is None else rows
    width = x.shape[1]
    return pl.pallas_call(
        _norm_kernel,
        grid=(rows // tm,),
        in_specs=[pl.BlockSpec((tm, width), lambda i: (i + row_blk0, 0)),
                  pl.BlockSpec((1, width), lambda i: (0, 0))],
        out_specs=pl.BlockSpec((tm, width), lambda i: (i, 0)),
        out_shape=jax.ShapeDtypeStruct((rows, width), out_dtype),
        compiler_params=_cparams(("parallel",), 40),
        name="rmsnorm",
    )(x, g.reshape(1, width).astype(f32))


def _two_group_specs(n_first, n_second, shape):
    return (pl.BlockSpec(shape, lambda i, *_: (jnp.minimum(i, n_first - 1), 0)),
            pl.BlockSpec(shape, lambda i, *_: (jnp.clip(i - n_first, 0, n_second - 1), 0)))


def _norm2_kernel(xp_ref, xs_ref, g_ref, o_ref, *, n_first):
    i = pl.program_id(0)

    @pl.when(i < n_first)
    def _():
        _norm_kernel(xp_ref, g_ref, o_ref)

    @pl.when(i >= n_first)
    def _():
        _norm_kernel(xs_ref, g_ref, o_ref)


def rmsnorm_stacked(xp, xs, g, *, tm, out_dtype):
    (mp, width), ms = xp.shape, xs.shape[0]
    n_first, n_second = mp // tm, ms // tm
    return pl.pallas_call(
        functools.partial(_norm2_kernel, n_first=n_first),
        grid=(n_first + n_second,),
        in_specs=[*_two_group_specs(n_first, n_second, (tm, width)),
                  pl.BlockSpec((1, width), lambda i: (0, 0))],
        out_specs=pl.BlockSpec((tm, width), lambda i: (i, 0)),
        out_shape=jax.ShapeDtypeStruct((mp + ms, width), out_dtype),
        compiler_params=_cparams(("arbitrary",), 40),
        name="rmsnorm_stacked",
    )(xp, xs, g.reshape(1, width).astype(f32))


def _mm_kernel(a_ref, w_ref, o_ref):
    o_ref[...] = _dot(a_ref[...], w_ref[...]).astype(o_ref.dtype)


def matmul(a, w, *, tm, tn, out_dtype, name="matmul"):
    m, k = a.shape
    n = w.shape[1]
    return pl.pallas_call(
        _mm_kernel,
        grid=(m // tm, n // tn),
        in_specs=[pl.BlockSpec((tm, k), lambda i, j: (i, 0)),
                  pl.BlockSpec((k, tn), lambda i, j: (0, j))],
        out_specs=pl.BlockSpec((tm, tn), lambda i, j: (i, j)),
        out_shape=jax.ShapeDtypeStruct((m, n), out_dtype),
        compiler_params=_cparams(("parallel", "arbitrary"), 56),
        name=name,
    )(a, w)


def _out_proj_kernel(op_ref, os_ref, yp_ref, ys_ref, w_ref, rp_ref, rs_ref, o_ref, *, n_first):
    i = pl.program_id(0)
    k1 = op_ref.shape[1]

    def project(a1_ref, a2_ref, r_ref):
        acc = _dot(a1_ref[...].astype(bf16), w_ref[:k1, :]) + _dot(a2_ref[...].astype(bf16), w_ref[k1:, :])
        o_ref[...] = acc + r_ref[...]

    @pl.when(i < n_first)
    def _():
        project(op_ref, yp_ref, rp_ref)

    @pl.when(i >= n_first)
    def _():
        project(os_ref, ys_ref, rs_ref)


def out_proj(o_p, y_p, o_s, y_s, w, res_p, res_s, *, tm, tn):
    (mp, k1), k2, ms = o_p.shape, y_p.shape[1], o_s.shape[0]
    n = w.shape[1]
    n_first, n_second = mp // tm, ms // tm
    res_p_spec = pl.BlockSpec((tm, tn), lambda i, j: (jnp.minimum(i, n_first - 1), j))
    res_s_spec = pl.BlockSpec((tm, tn), lambda i, j: (jnp.clip(i - n_first, 0, n_second - 1), j))
    return pl.pallas_call(
        functools.partial(_out_proj_kernel, n_first=n_first),
        grid=(n_first + n_second, n // tn),
        in_specs=[*_two_group_specs(n_first, n_second, (tm, k1)), *_two_group_specs(n_first, n_second, (tm, k2)),
                  pl.BlockSpec((k1 + k2, tn), lambda i, j: (0, j)), res_p_spec, res_s_spec],
        out_specs=pl.BlockSpec((tm, tn), lambda i, j: (i, j)),
        out_shape=jax.ShapeDtypeStruct((mp + ms, n), f32),
        compiler_params=_cparams(("arbitrary", "arbitrary"), 56),
        name="out_proj",
    )(o_p, o_s, y_p, y_s, w, res_p, res_s)


def _ffn_up_kernel(a_ref, wg_ref, wu_ref, o_ref):
    a = a_ref[...]
    g = _dot(a, wg_ref[0].astype(bf16))
    u = _dot(a, wu_ref[0].astype(bf16))
    o_ref[...] = (g * _sigmoid(g) * u).astype(o_ref.dtype)


def ffn_up(a, wg, wu, layer, *, tm, tn):
    m, k = a.shape
    n = wg.shape[2]
    wspec = pl.BlockSpec((1, k, tn), lambda i, j: (layer, 0, j))
    return pl.pallas_call(
        _ffn_up_kernel,
        grid=(m // tm, n // tn),
        in_specs=[pl.BlockSpec((tm, k), lambda i, j: (i, 0)), wspec, wspec],
        out_specs=pl.BlockSpec((tm, tn), lambda i, j: (i, j)),
        out_shape=jax.ShapeDtypeStruct((m, n), bf16),
        compiler_params=_cparams(("parallel", "arbitrary"), 56),
        name="ffn_up",
    )(a, wg, wu)


def _ffn_down_kernel(a_ref, w_ref, r_ref, o_ref):
    o_ref[...] = _dot(a_ref[...], w_ref[0]) + r_ref[...]


def ffn_down(a, w, layer, residual, *, tm, tn):
    m, k = a.shape
    n = w.shape[2]
    return pl.pallas_call(
        _ffn_down_kernel,
        grid=(m // tm, n // tn),
        in_specs=[pl.BlockSpec((tm, k), lambda i, j: (i, 0)),
                  pl.BlockSpec((1, k, tn), lambda i, j: (layer, 0, j)),
                  pl.BlockSpec((tm, tn), lambda i, j: (i, j))],
        out_specs=pl.BlockSpec((tm, tn), lambda i, j: (i, j)),
        out_shape=jax.ShapeDtypeStruct((m, n), f32),
        compiler_params=_cparams(("parallel", "arbitrary"), 56),
        name="ffn_down",
    )(a, w, residual)


def _pool_mm_kernel(ap_ref, as_ref, w_ref, s_ref, r_ref, o_ref, *, n_first):
    i = pl.program_id(0)

    def project(a_ref):
        o_ref[...] = _dot(a_ref[...], w_ref[0]) * s_ref[...] + r_ref[...]

    @pl.when(i < n_first)
    def _():
        project(ap_ref)

    @pl.when(i >= n_first)
    def _():
        project(as_ref)


def pool_matmul(mix_p, mix_s, w, scale, residual, *, tm, tn):
    (mp, d), ms = mix_p.shape, mix_s.shape[0]
    ng, gd, _ = w.shape
    nj = gd // tn
    n_first, n_second = mp // tm, ms // tm
    return pl.pallas_call(
        functools.partial(_pool_mm_kernel, n_first=n_first),
        grid=(n_first + n_second, ng, nj),
        in_specs=[pl.BlockSpec((tm, gd), lambda i, g, j: (jnp.minimum(i, n_first - 1), g)),
                  pl.BlockSpec((tm, gd), lambda i, g, j: (jnp.clip(i - n_first, 0, n_second - 1), g)),
                  pl.BlockSpec((1, gd, tn), lambda i, g, j: (g, 0, j)),
                  pl.BlockSpec((1, tn), lambda i, g, j: (0, g * nj + j)),
                  pl.BlockSpec((tm, tn), lambda i, g, j: (i, g * nj + j))],
        out_specs=pl.BlockSpec((tm, tn), lambda i, g, j: (i, g * nj + j)),
        out_shape=jax.ShapeDtypeStruct((mp + ms, d), f32),
        compiler_params=_cparams(("arbitrary", "arbitrary", "arbitrary"), 40),
        name="pool_matmul",
    )(mix_p, mix_s, w, scale.reshape(1, d).astype(f32), residual)


def _kv_prep_kernel(kv_ref, kr_ref, krr_ref, cos_ref, sin_ref, g_ref, ckv_ref, kpe_ref, kcat_ref):
    x = kv_ref[...]
    ms = jnp.mean(x * x, axis=-1, keepdims=True)
    ckv = x * lax.rsqrt(ms + RMS_EPS) * g_ref[...]
    kpe = kr_ref[...] * cos_ref[...] + krr_ref[...] * sin_ref[...]
    ckv_ref[...] = ckv
    kpe_ref[...] = kpe
    kcat_ref[:, :KV_LORA] = ckv.astype(bf16)
    kcat_ref[:, KV_LORA:] = kpe.astype(bf16)


def kv_prep(proj, cos_t, sin_t, kv_norm_g, *, tm):
    m = proj.shape[0]
    row = lambda w, cb: pl.BlockSpec((tm, w), lambda i: (i, cb))
    return pl.pallas_call(
        _kv_prep_kernel,
        grid=(m // tm,),
        in_specs=[row(KV_LORA, COL_KV // KV_LORA), row(ROPE_PAD, COL_KR // ROPE_PAD),
                  row(ROPE_PAD, COL_KRR // ROPE_PAD), row(ROPE_PAD, 0), row(ROPE_PAD, 0),
                  pl.BlockSpec((1, KV_LORA), lambda i: (0, 0))],
        out_specs=[row(KV_LORA, 0), row(ROPE_PAD, 0), row(QK_CAT, 0)],
        out_shape=[jax.ShapeDtypeStruct((m, KV_LORA), f32), jax.ShapeDtypeStruct((m, ROPE_PAD), f32),
                   jax.ShapeDtypeStruct((m, QK_CAT), bf16)],
        compiler_params=_cparams(("parallel",), 32),
        name="kv_prep",
    )(proj, proj, proj, cos_t, sin_t, kv_norm_g.reshape(1, KV_LORA).astype(f32))


def _q_prep_kernel(cq_ref, g_ref, wn_ref, wr_ref, wrr_ref, wuk_ref, cos_ref, sin_ref, q_ref, cqn_sc):
    @pl.when(pl.program_id(1) == 0)
    def _():
        x = cq_ref[...]
        ms = jnp.mean(x * x, axis=-1, keepdims=True)
        cqn_sc[...] = (x * lax.rsqrt(ms + RMS_EPS) * g_ref[...]).astype(bf16)

    cqn = cqn_sc[...]
    q_nope = _dot(cqn, wn_ref[0]).astype(bf16)
    q_r = _dot(cqn, wr_ref[0])
    q_rr = _dot(cqn, wrr_ref[0])
    for e in range(wuk_ref.shape[0]):
        q_lat = _dot(q_nope[:, e * QK_NOPE:(e + 1) * QK_NOPE], wuk_ref[e])
        cols = slice(e * ROPE_PAD, (e + 1) * ROPE_PAD)
        q_pe = q_r[:, cols] * cos_ref[...] + q_rr[:, cols] * sin_ref[...]
        q_ref[e, :, :KV_LORA] = (q_lat * ATTN_SCALE).astype(q_ref.dtype)
        q_ref[e, :, KV_LORA:] = (q_pe * ATTN_SCALE).astype(q_ref.dtype)


def _side_by_side(w, hps):
    nh, k, n = w.shape
    return w.reshape(nh // hps, hps, k, n).transpose(0, 2, 1, 3).reshape(nh // hps, k, hps * n)


def q_prep(proj, cos_t, sin_t, q_norm_g, wn, wr, wrr, wuk, *, rows, row_blk0, tm, out_dtype, hps=2):
    nh = wn.shape[0]
    hw = lambda a, b: pl.BlockSpec((1, a, hps * b), lambda i, h: (h, 0, 0))
    return pl.pallas_call(
        _q_prep_kernel,
        grid=(rows // tm, nh // hps),
        in_specs=[pl.BlockSpec((tm, Q_LORA), lambda i, h: (i + row_blk0, COL_Q // Q_LORA)),
                  pl.BlockSpec((1, Q_LORA), lambda i, h: (0, 0)),
                  hw(Q_LORA, QK_NOPE), hw(Q_LORA, ROPE_PAD), hw(Q_LORA, ROPE_PAD),
                  pl.BlockSpec((hps, QK_NOPE, KV_LORA), lambda i, h: (h, 0, 0)),
                  pl.BlockSpec((tm, ROPE_PAD), lambda i, h: (i + row_blk0, 0)),
                  pl.BlockSpec((tm, ROPE_PAD), lambda i, h: (i + row_blk0, 0))],
        out_specs=pl.BlockSpec((hps, tm, QK_CAT), lambda i, h: (h, i, 0)),
        out_shape=jax.ShapeDtypeStruct((nh, rows, QK_CAT), out_dtype),
        scratch_shapes=[pltpu.VMEM((tm, Q_LORA), bf16)],
        compiler_params=_cparams(("parallel", "arbitrary"), 40),
        name="q_prep",
    )(proj, q_norm_g.reshape(1, Q_LORA).astype(f32), _side_by_side(wn, hps), _side_by_side(wr, hps),
      _side_by_side(wrr, hps), wuk, cos_t, sin_t)


def _attn_prompt_kernel(q_ref, k_ref, wuv_ref, o_ref, m_sc, l_sc, acc_sc, *, tq, tk, row_chunks):
    nh = q_ref.shape[0]
    hpc = nh // row_chunks
    rc = hpc * tq
    qb = pl.program_id(1)
    m_sc[...] = jnp.full(m_sc.shape, -1e30, f32)
    l_sc[...] = jnp.zeros(l_sc.shape, f32)
    acc_sc[...] = jnp.zeros(acc_sc.shape, f32)

    def step(kb, masked, width=tk):
        k = k_ref[0, pl.ds(pl.multiple_of(kb * tk, tk), width), :]
        v = k[:, :KV_LORA]
        for r in range(row_chunks):
            rs = slice(r * rc, (r + 1) * rc)
            q = q_ref[r * hpc:(r + 1) * hpc].reshape(rc, QK_CAT)
            s = _dot_nt(q, k)
            if masked:
                qpos = qb * tq + lax.broadcasted_iota(jnp.int32, (hpc, tq, width), 1).reshape(rc, width)
                kpos = kb * tk + lax.broadcasted_iota(jnp.int32, (rc, width), 1)
                s = jnp.where(kpos <= qpos, s, -jnp.inf)
            m_prev = m_sc[rs]
            m_new = jnp.maximum(m_prev, jnp.max(s, axis=-1, keepdims=True))
            alpha = jnp.exp(m_prev - m_new)
            p = jnp.exp(s - m_new)
            l_sc[rs] = alpha * l_sc[rs] + jnp.sum(p, axis=-1, keepdims=True)
            acc_sc[rs] = alpha * acc_sc[rs] + _dot(p.astype(bf16), v)
            m_sc[rs] = m_new

    n_full = (qb * tq) // tk

    def body(kb, carry):
        step(kb, False)
        return carry

    lax.fori_loop(0, n_full, body, 0)
    if tk % (2 * tq) == 0:
        first_half = (qb + 1) * tq - n_full * tk <= tk // 2

        @pl.when(first_half)
        def _():
            step(n_full, True, tk // 2)

        @pl.when(jnp.logical_not(first_half))
        def _():
            step(n_full, True)
    else:
        step(n_full, True)

    o = (acc_sc[...] / l_sc[...]).astype(bf16).reshape(nh, tq, KV_LORA)
    for h in range(nh):
        o_ref[0, :, h * V_DIM:(h + 1) * V_DIM] = _dot(o[h], wuv_ref[h]).astype(o_ref.dtype)


def attn_prompt(q, kcat, wuv, *, nb, seq, tq, tk, row_chunks):
    nh = q.shape[0]
    assert tk % tq == 0 and seq % tk == 0 and nh % row_chunks == 0
    nqb = seq // tq
    rows = nh * tq
    kern = functools.partial(_attn_prompt_kernel, tq=tq, tk=tk, row_chunks=row_chunks)
    return pl.pallas_call(
        kern,
        grid=(nb, nqb),
        in_specs=[pl.BlockSpec((nh, tq, QK_CAT), lambda b, i: (0, b * nqb + i, 0)),
                  pl.BlockSpec((1, seq, QK_CAT), lambda b, i: (b, 0, 0)),
                  pl.BlockSpec((nh, KV_LORA, V_DIM), lambda b, i: (0, 0, 0))],
        out_specs=pl.BlockSpec((1, tq, nh * V_DIM), lambda b, i: (b, i, 0)),
        out_shape=jax.ShapeDtypeStruct((nb, seq, nh * V_DIM), bf16),
        scratch_shapes=[pltpu.VMEM((rows, 1), f32), pltpu.VMEM((rows, 1), f32),
                        pltpu.VMEM((rows, KV_LORA), f32)],
        compiler_params=_cparams(("parallel", "arbitrary"), 48),
        name="attn_prompt",
    )(q, kcat, wuv)


def _attn_sample_kernel(pt_ref, q_ref, slat_ref, spe_ref, wuv_ref, clat_hbm, crope_hbm, o_ref,
                        lat_buf, rope_buf, sem, m_sc, l_sc, acc_sc, q_sc, *, li, ch, n_sub, t_new):
    nh = q_ref.shape[0]
    rows = nh * t_new
    n_chunks = pt_ref.shape[1] // ch
    ps = ch // n_sub
    b = pl.program_id(0)
    nb = pl.num_programs(0)

    def page_copies(pid, slot, j):
        return (pltpu.make_async_copy(clat_hbm.at[li, pid], lat_buf.at[slot, j], sem.at[0, slot]),
                pltpu.make_async_copy(crope_hbm.at[li, pid], rope_buf.at[slot, j], sem.at[1, slot]))

    def start_chunk(bb, c, slot):
        for j in range(ch):
            for cp in page_copies(pt_ref[bb, c * ch + j], slot, j):
                cp.start()

    def wait_chunk(slot):
        for j in range(ch):
            for cp in page_copies(0, slot, j):
                cp.wait()

    n_slots = lat_buf.shape[0]
    ahead = n_slots - 1

    def chunk_at(g):
        return b + g // n_chunks, g % n_chunks

    @pl.when(b == 0)
    def _():
        for g in range(ahead):
            start_chunk(g // n_chunks, g % n_chunks, g % n_slots)

    q = q_ref[...].reshape(rows, QK_CAT).astype(bf16)
    q_sc[...] = q
    pad = 2 * SUBLANES - t_new
    ks = jnp.concatenate([jnp.concatenate([slat_ref[...], spe_ref[...]], axis=1),
                          jnp.zeros((pad, QK_CAT), f32)], axis=0).astype(bf16)
    s = _dot_nt(q, ks)
    tpos = lax.broadcasted_iota(jnp.int32, (nh, t_new, 2 * SUBLANES), 1).reshape(rows, 2 * SUBLANES)
    kpos = lax.broadcasted_iota(jnp.int32, (rows, 2 * SUBLANES), 1)
    s = jnp.where(kpos <= tpos, s, -jnp.inf)
    m0 = jnp.maximum(jnp.max(s, axis=-1, keepdims=True), -1e30)
    p = jnp.exp(s - m0)
    m_sc[...] = m0
    l_sc[...] = jnp.sum(p, axis=-1, keepdims=True)
    acc_sc[...] = _dot(p.astype(bf16), ks[:, :KV_LORA])

    def softmax_part(slot, sub):
        k = lat_buf[slot, sub * ps:(sub + 1) * ps].reshape(ps * PAGE_SIZE, KV_LORA).astype(bf16)
        kr_t = jnp.concatenate([rope_buf[slot, sub * ps + j] for j in range(ps)], axis=1).astype(bf16)
        s = _dot_nt(q_sc[:, :KV_LORA], k) + _dot(q_sc[:, KV_LORA:KV_LORA + QK_ROPE], kr_t)
        m_s = jnp.max(s, axis=-1, keepdims=True)
        p = jnp.exp(s - m_s)
        return m_s, jnp.sum(p, axis=-1, keepdims=True), _dot(p.astype(bf16), k)

    def merge(parts):
        m_old = m_sc[...]
        m_new = m_old
        for m_s, _, _ in parts:
            m_new = jnp.maximum(m_new, m_s)
        alpha = jnp.exp(m_old - m_new)
        l = alpha * l_sc[...]
        acc = alpha * acc_sc[...]
        for m_s, l_s, o_s in parts:
            w = jnp.exp(m_s - m_new)
            l = l + w * l_s
            acc = acc + w * o_s
        m_sc[...] = m_new
        l_sc[...] = l
        acc_sc[...] = acc

    def chunk(c, carry):
        g = b * n_chunks + c
        slot = g % n_slots
        wait_chunk(slot)
        bb, cc = chunk_at(c + ahead)

        @pl.when(bb < nb)
        def _():
            start_chunk(bb, cc, (g + ahead) % n_slots)

        merge([softmax_part(slot, sub) for sub in range(n_sub)])
        return carry

    lax.fori_loop(0, n_chunks, chunk, 0)

    o = (acc_sc[...] / l_sc[...]).astype(bf16).astype(f32).reshape(nh, t_new, KV_LORA)
    for h in range(nh):
        o_ref[0, :, h * V_DIM:(h + 1) * V_DIM] = _dot(o[h], wuv_ref[h].astype(f32))


def attn_sample(page_table, q, ckv, kpe, cache_lat, cache_rope_t, wuv, *, li, row0, t_new, ch, n_sub, n_slots):
    nh = q.shape[0]
    nb, n_pages = page_table.shape
    assert n_pages % ch == 0 and ch % n_sub == 0 and row0 % t_new == 0
    assert 2 <= n_slots <= nb * (n_pages // ch)
    rows = nh * t_new
    blk0 = row0 // t_new
    kern = functools.partial(_attn_sample_kernel, li=li, ch=ch, n_sub=n_sub, t_new=t_new)
    grid_spec = pltpu.PrefetchScalarGridSpec(
        num_scalar_prefetch=1,
        grid=(nb,),
        in_specs=[pl.BlockSpec((nh, t_new, QK_CAT), lambda b, pt: (0, b, 0)),
                  pl.BlockSpec((t_new, KV_LORA), lambda b, pt: (blk0 + b, 0)),
                  pl.BlockSpec((t_new, ROPE_PAD), lambda b, pt: (blk0 + b, 0)),
                  pl.BlockSpec((nh, KV_LORA, V_DIM), lambda b, pt: (0, 0, 0)),
                  pl.BlockSpec(memory_space=pl.ANY),
                  pl.BlockSpec(memory_space=pl.ANY)],
        out_specs=pl.BlockSpec((1, t_new, nh * V_DIM), lambda b, pt: (b, 0, 0)),
        scratch_shapes=[pltpu.VMEM((n_slots, ch, PAGE_SIZE, KV_LORA), f32),
                        pltpu.VMEM((n_slots, ch, QK_ROPE, PAGE_SIZE), f32),
                        pltpu.SemaphoreType.DMA((2, n_slots)),
                        pltpu.VMEM((rows, 1), f32), pltpu.VMEM((rows, 1), f32),
                        pltpu.VMEM((rows, KV_LORA), f32),
                        pltpu.VMEM((rows, QK_CAT), bf16)],
    )
    return pl.pallas_call(
        kern,
        grid_spec=grid_spec,
        out_shape=jax.ShapeDtypeStruct((nb, t_new, nh * V_DIM), f32),
        compiler_params=_cparams(("arbitrary",), 40),
        name="attn_sample",
    )(page_table, q, ckv, kpe, wuv, cache_lat, cache_rope_t)


def _expand_heads(cols, n):
    q = cols.shape[0]
    lane_head = lax.broadcasted_iota(jnp.int32, (q, GROUP_P), 1) // SSM_HEAD_DIM
    out = jnp.broadcast_to(cols[:, 0:1], (q, GROUP_P))
    for r in range(1, n):
        out = jnp.where(lane_head == r, cols[:, r:r + 1], out)
    return out


def _ssd_kernel(xbc_ref, z_ref, dt_ref, hist_ref, h0_ref, cw_ref, cb_ref, dtb_ref, a_ref, dsk_ref, ng_ref,
                y_ref, hout_ref, ext_sc, h_sc, y_sc, *, q, seqs_per_step, single_chunk, mxu_dtype):
    for bi in range(seqs_per_step):
        rows = pl.ds(bi * q, q)
        _ssd_sequence(xbc_ref.at[rows], z_ref.at[rows], dt_ref.at[rows], hist_ref.at[bi], h0_ref.at[bi],
                      cw_ref, cb_ref, dtb_ref, a_ref, dsk_ref, ng_ref, y_ref.at[rows], hout_ref.at[bi],
                      ext_sc.at[bi], h_sc.at[bi], y_sc.at[bi], q=q, single_chunk=single_chunk,
                      mxu_dtype=mxu_dtype)


def _ssd_sequence(xbc_ref, z_ref, dt_ref, hist_ref, h0_ref, cw_ref, cb_ref, dtb_ref, a_ref, dsk_ref, ng_ref,
                  y_ref, hout_ref, ext_sc, h_sc, y_sc, *, q, single_chunk, mxu_dtype):
    c = pl.program_id(1)
    cast = lambda v: v.astype(mxu_dtype)
    when = (lambda cond: (lambda fn: fn())) if single_chunk else pl.when

    @when(c == 0)
    def _():
        ext_sc[CONV_HALO - (CONV_W - 1):CONV_HALO, :] = hist_ref[...]
        h_sc[...] = h0_ref[...]

    ext_sc[CONV_HALO:CONV_HALO + q, :] = xbc_ref[...]
    conv = cb_ref[...]
    for k in range(CONV_W):
        conv = conv + ext_sc[pl.ds(CONV_HALO - (CONV_W - 1) + k, q), :] * cw_ref[k:k + 1, :]
    ext_sc[0:CONV_HALO, :] = ext_sc[q:q + CONV_HALO, :]
    act = conv * _sigmoid(conv)

    dt_in = dt_ref[...] + dtb_ref[...]
    dt = jnp.maximum(dt_in, 0.0) + jnp.log1p(jnp.exp(-jnp.abs(dt_in)))
    dta = dt * a_ref[...]
    ri = lax.broadcasted_iota(jnp.int32, (q, q), 0)
    ci = lax.broadcasted_iota(jnp.int32, (q, q), 1)
    tril = ci <= ri
    trif = tril.astype(f32)
    eye_l = (lax.broadcasted_iota(jnp.int32, (LANES, LANES), 0)
             == lax.broadcasted_iota(jnp.int32, (LANES, LANES), 1)).astype(f32)
    eye_p = (lax.broadcasted_iota(jnp.int32, (GROUP_P, GROUP_P), 0)
             == lax.broadcasted_iota(jnp.int32, (GROUP_P, GROUP_P), 1)).astype(mxu_dtype)
    a_cum = sum(_dot(trif, part) for part in _split3(dta))
    a_cum_t = sum(_dot_nt(eye_l, part) for part in _split3(a_cum))
    dt_t = sum(_dot_nt(eye_l, part) for part in _split3(dt))
    a_last = a_cum[q - 1:q, :]
    decay = jnp.exp(a_last - a_cum) * dt
    e_acum = jnp.exp(a_cum)
    e_last_t = jnp.exp(a_cum_t[:, q - 1:q])
    lane_head = lax.broadcasted_iota(jnp.int32, (q, GROUP_P), 1) // SSM_HEAD_DIM

    for g in range(SSM_GROUPS):
        h_lo = g * HEADS_PER_GROUP
        bg = cast(act[:, SSM_D_INNER + g * SSM_D_STATE:SSM_D_INNER + (g + 1) * SSM_D_STATE])
        cg = cast(act[:, SSM_D_INNER + (SSM_GROUPS + g) * SSM_D_STATE:
                      SSM_D_INNER + (SSM_GROUPS + g + 1) * SSM_D_STATE])
        xg = act[:, g * GROUP_P:(g + 1) * GROUP_P]
        xg_m = cast(xg)
        cb = _dot_nt(cg, bg)
        hg = h_sc[g]
        y_off = _dot_nt(cg, cast(hg))
        y_diag = jnp.zeros((q, GROUP_P), f32)
        for r in range(HEADS_PER_GROUP):
            h = h_lo + r
            seg = a_cum[:, h:h + 1] - a_cum_t[h:h + 1, :]
            w = cb * jnp.exp(jnp.where(tril, seg, -jnp.inf)) * dt_t[h:h + 1, :]
            y_diag = jnp.where(lane_head == r, _dot(cast(w), xg_m), y_diag)
        e_g = _expand_heads(e_acum[:, h_lo:h_lo + HEADS_PER_GROUP], HEADS_PER_GROUP)
        d_g = _expand_heads(decay[:, h_lo:h_lo + HEADS_PER_GROUP], HEADS_PER_GROUP)
        y_sc[:, g * GROUP_P:(g + 1) * GROUP_P] = (
            y_diag + y_off * e_g + dsk_ref[:, g * GROUP_P:(g + 1) * GROUP_P] * xg)
        xdec_t = cast(_dot_nt(eye_p, cast(xg * d_g)))
        states = _dot(xdec_t, bg)
        cdec = jnp.concatenate(
            [jnp.broadcast_to(e_last_t[h_lo + r:h_lo + r + 1, :], (SSM_HEAD_DIM, SSM_D_STATE))
             for r in range(HEADS_PER_GROUP)], axis=0)
        h_sc[g] = hg * cdec + states

    gated = y_sc[...] * (z_ref[...] * _sigmoid(z_ref[...]))
    ms = jnp.mean(gated * gated, axis=-1, keepdims=True)
    y_ref[...] = (gated * lax.rsqrt(ms + RMS_EPS) * ng_ref[...]).astype(y_ref.dtype)

    @when(c == pl.num_programs(1) - 1)
    def _():
        hout_ref[...] = h_sc[...]


def ssd_mixer(proj, hist, h0, conv_w, conv_b, dt_bias, a_log, d_skip, norm_g, *, nb, seq, q, row0,
              mxu_dtype, out_dtype, seqs_per_step=1):
    sps = seqs_per_step
    nc = seq // q
    assert seq % q == 0 and row0 % (sps * q) == 0 and q >= CONV_HALO and nb % sps == 0
    assert sps == 1 or nc == 1
    blk0 = row0 // (sps * q)
    pad_l = lambda v: jnp.pad(v.astype(f32), (0, LANES - v.shape[0])).reshape(1, LANES)
    a = -jnp.exp(a_log.astype(f32))
    row = lambda w, cb: pl.BlockSpec((sps * q, w), lambda b, c: (blk0 + b * nc + c, cb))
    const = lambda r, w: pl.BlockSpec((r, w), lambda b, c: (0, 0))
    kern = functools.partial(_ssd_kernel, q=q, seqs_per_step=sps, single_chunk=nc == 1, mxu_dtype=mxu_dtype)
    state_spec = pl.BlockSpec((sps, SSM_GROUPS, GROUP_P, SSM_D_STATE), lambda b, c: (b, 0, 0, 0))
    return pl.pallas_call(
        kern,
        grid=(nb // sps, nc),
        in_specs=[row(CONV_DIM, COL_XBC // CONV_DIM), row(SSM_D_INNER, COL_Z // SSM_D_INNER),
                  row(LANES, COL_DT // LANES),
                  pl.BlockSpec((sps, CONV_W - 1, CONV_DIM), lambda b, c: (b, 0, 0)), state_spec,
                  const(CONV_W, CONV_DIM), const(1, CONV_DIM), const(1, LANES), const(1, LANES),
                  const(1, SSM_D_INNER), const(1, SSM_D_INNER)],
        out_specs=[pl.BlockSpec((sps * q, SSM_D_INNER), lambda b, c: (b * nc + c, 0)), state_spec],
        out_shape=[jax.ShapeDtypeStruct((nb * seq, SSM_D_INNER), out_dtype),
                   jax.ShapeDtypeStruct((nb, SSM_GROUPS, GROUP_P, SSM_D_STATE), f32)],
        scratch_shapes=[pltpu.VMEM((sps, q + CONV_HALO, CONV_DIM), f32),
                        pltpu.VMEM((sps, SSM_GROUPS, GROUP_P, SSM_D_STATE), f32),
                        pltpu.VMEM((sps, q, SSM_D_INNER), f32)],
        compiler_params=_cparams(("parallel", "arbitrary"), 40),
        name="ssd_mixer",
    )(proj, proj, proj, hist, h0, conv_w.astype(f32), conv_b.reshape(1, CONV_DIM).astype(f32),
      pad_l(dt_bias), pad_l(a), jnp.repeat(d_skip.astype(f32), SSM_HEAD_DIM).reshape(1, SSM_D_INNER),
      norm_g.reshape(1, SSM_D_INNER).astype(f32))


def _pool_mix_kernel(cur_ref, halo_ref, o_ref, ext_sc, *, tm, seqs_per_step, pos0, zero_first_halo):
    i = pl.program_id(1)
    gd = cur_ref.shape[1] // len(POOL_WINDOWS)
    count_pos = (pos0 + i * tm + 1 + lax.broadcasted_iota(jnp.int32, (tm, 1), 0)).astype(f32)
    for bi in range(seqs_per_step):
        rows = slice(bi * tm, (bi + 1) * tm)
        ext = ext_sc.at[bi]
        if zero_first_halo:
            ext[0:HALO, :] = jnp.where(i == 0, 0.0, halo_ref[...])
        else:
            ext[HALO - POOL_HIST:HALO, :] = halo_ref[bi]
        ext[HALO:HALO + tm, :] = cur_ref[rows, :]
        for gi, win in enumerate(POOL_WINDOWS):
            cols = slice(gi * gd, (gi + 1) * gd)
            u = ext[HALO:HALO + tm, cols]
            wsum = u
            for k in range(1, win):
                wsum = wsum + ext[pl.ds(HALO - k, tm), cols]
            mean = wsum / jnp.minimum(count_pos, float(win))
            o_ref[rows, cols] = (mean - u).astype(o_ref.dtype)


def pool_mix(u, halo, *, nb, seq, tm, row0, pos0, halo_from_u, seqs_per_step=1):
    d = u.shape[1]
    sps = seqs_per_step
    nt = seq // tm
    blk0 = row0 // (sps * tm)
    assert row0 % (sps * tm) == 0 and nb % sps == 0 and (halo_from_u or nt == 1) and (sps == 1 or nt == 1)
    if halo_from_u:
        assert tm % HALO == 0 and sps == 1
        hb = tm // HALO
        halo_spec = pl.BlockSpec((HALO, d), lambda b, i: (jnp.maximum((blk0 + b * nt + i) * hb - 1, 0), 0))
        halo = u
    else:
        halo_spec = pl.BlockSpec((sps, POOL_HIST, d), lambda b, i: (b, 0, 0))
    kern = functools.partial(_pool_mix_kernel, tm=tm, seqs_per_step=sps, pos0=pos0, zero_first_halo=halo_from_u)
    return pl.pallas_call(
        kern,
        grid=(nb // sps, nt),
        in_specs=[pl.BlockSpec((sps * tm, d), lambda b, i: (blk0 + b * nt + i, 0)), halo_spec],
        out_specs=pl.BlockSpec((sps * tm, d), lambda b, i: (b * nt + i, 0)),
        out_shape=jax.ShapeDtypeStruct((nb * seq, d), bf16),
        scratch_shapes=[pltpu.VMEM((sps, HALO + tm, d), f32)],
        compiler_params=_cparams(("parallel", "arbitrary"), 40),
        name="pool_mix",
    )(u, halo)


def _rope_tables(pos):
    inv_freq = 1.0 / (ROPE_THETA ** (jnp.arange(0, QK_ROPE, 2, dtype=f32) / QK_ROPE))
    ang = pos.astype(f32)[:, None] * inv_freq[None, :]
    cos, sin = jnp.cos(ang), jnp.sin(ang)
    zeros = jnp.zeros((pos.shape[0], ROPE_PAD - QK_ROPE), f32)
    return (jnp.concatenate([cos, cos, zeros], axis=1), jnp.concatenate([-sin, sin, zeros], axis=1))


def _swap_halves(w):
    half = w.shape[-1] // 2
    return jnp.concatenate([w[..., half:], w[..., :half]], axis=-1)


def _pad_last(w, n):
    return jnp.pad(w, [(0, 0)] * (w.ndim - 1) + [(0, n - w.shape[-1])])


def kernel(x_prompt, x_sample, cache_mla_latent, cache_mla_rope, state_ssm_conv, state_ssm, state_pool, page_table, norm_mix_g, norm_ffn_g, norm_final_g, w_in, q_norm_g, kv_norm_g, w_uq, w_ukv, conv_w, conv_b, dt_bias, a_log, d_skip, ssm_norm_g, w_out, w_pool, pool_scale, w_gate, w_up, w_down):
    bp, lp, d_model = x_prompt.shape
    bs, ls, _ = x_sample.shape
    mp, ms_rows = bp * lp, bs * ls
    li = 0

    wi = w_in[li]
    o_kv = Q_LORA
    o_kr = o_kv + KV_LORA
    o_z = o_kr + QK_ROPE
    o_xbc = o_z + SSM_D_INNER
    o_dt = o_xbc + CONV_DIM
    w_kr = wi[:, o_kr:o_z]
    w_in_r = jnp.concatenate(
        [wi[:, o_xbc:o_dt], wi[:, o_z:o_xbc], wi[:, :o_kv], wi[:, o_kv:o_kr],
         _pad_last(w_kr, ROPE_PAD), _pad_last(_swap_halves(w_kr), ROPE_PAD),
         _pad_last(wi[:, o_dt:], 2 * LANES)], axis=1).astype(bf16)
    assert w_in_r.shape[1] == PROJ_W
    wq = w_uq[li]
    wq_nope = wq[:, :, :QK_NOPE].transpose(1, 0, 2).astype(bf16)
    wq_rope = wq[:, :, QK_NOPE:].transpose(1, 0, 2)
    wq_r = _pad_last(wq_rope, ROPE_PAD).astype(bf16)
    wq_rr = _pad_last(_swap_halves(wq_rope), ROPE_PAD).astype(bf16)
    wkv = w_ukv[li]
    w_uk = wkv[:, :, :QK_NOPE].transpose(1, 2, 0).astype(bf16)
    w_uv = wkv[:, :, QK_NOPE:].transpose(1, 0, 2).astype(bf16)
    w_out_b = w_out[li].astype(bf16)
    w_pool_b = w_pool[0].astype(bf16)
    w_down_b = w_down.astype(bf16)

    pos = jnp.concatenate([jnp.tile(jnp.arange(lp), bp), jnp.tile(PAST_LEN + jnp.arange(ls), bs)])
    cos_t, sin_t = _rope_tables(pos)

    xp2 = x_prompt.reshape(mp, d_model)
    xs2 = x_sample.reshape(ms_rows, d_model)

    u = rmsnorm_stacked(xp2, xs2, norm_mix_g[0], tm=256, out_dtype=bf16)
    proj = matmul(u, w_in_r, tm=1536, tn=512, out_dtype=f32, name="in_proj")
    ckv, kpe, kcat = kv_prep(proj, cos_t, sin_t, kv_norm_g[li], tm=512)

    q_w = (q_norm_g[li], wq_nope, wq_r, wq_rr, w_uk)
    q_p = q_prep(proj, cos_t, sin_t, *q_w, rows=mp, row_blk0=0, tm=1024, out_dtype=bf16)
    q_s = q_prep(proj, cos_t, sin_t, *q_w, rows=ms_rows, row_blk0=mp // 1024, tm=1024, out_dtype=f32)
    o_p = attn_prompt(q_p, kcat[:mp].reshape(bp, lp, QK_CAT), w_uv, nb=bp, seq=lp, tq=256, tk=512, row_chunks=2)
    o_s = attn_sample(page_table, q_s, ckv, kpe, cache_mla_latent, jnp.swapaxes(cache_mla_rope, 2, 3), w_uv,
                      li=li, row0=mp, t_new=ls, ch=16, n_sub=4, n_slots=3)

    hist_p = jnp.zeros((bp, CONV_W - 1, CONV_DIM), f32)
    h0_p = jnp.zeros((bp, SSM_GROUPS, GROUP_P, SSM_D_STATE), f32)
    h0_s = state_ssm[li].reshape(bs, SSM_GROUPS, GROUP_P, SSM_D_STATE)
    ssd_w = (conv_w[li], conv_b[li], dt_bias[li], a_log[li], d_skip[li], ssm_norm_g[li])
    y_p, h_p = ssd_mixer(proj, hist_p, h0_p, *ssd_w, nb=bp, seq=lp, q=SSD_CHUNK, row0=0,
                         mxu_dtype=bf16, out_dtype=bf16)
    y_s, h_s = ssd_mixer(proj, state_ssm_conv[li], h0_s, *ssd_w, nb=bs, seq=ls, q=ls, row0=mp,
                         mxu_dtype=f32, out_dtype=f32, seqs_per_step=2)

    x1 = out_proj(o_p.reshape(mp, -1), y_p, o_s.reshape(ms_rows, -1).astype(bf16), y_s.astype(bf16), w_out_b,
                  xp2, xs2, tm=1024, tn=512)

    def ffn(x, layer):
        uf = rmsnorm(x, norm_ffn_g[layer], tm=256, out_dtype=bf16)
        hid = ffn_up(uf, w_gate, w_up, layer, tm=1536, tn=256)
        return ffn_down(hid, w_down_b, layer, x, tm=512, tn=512)

    x2 = ffn(x1, 0)

    u1 = rmsnorm(x2, norm_mix_g[1], tm=256, out_dtype=f32)
    mix_p = pool_mix(u1, None, nb=bp, seq=lp, tm=256, row0=0, pos0=0, halo_from_u=True)
    mix_s = pool_mix(u1, state_pool[0], nb=bs, seq=ls, tm=ls, row0=mp, pos0=PAST_LEN, halo_from_u=False,
                     seqs_per_step=8)
    x3 = pool_matmul(mix_p, mix_s, w_pool_b, pool_scale[0], x2, tm=1024, tn=512)
    x4 = ffn(x3, 1)

    y_prompt = rmsnorm(x4, norm_final_g, tm=256, out_dtype=f32, rows=mp).reshape(bp, lp, d_model)
    y_sample = rmsnorm(x4, norm_final_g, tm=256, out_dtype=f32, rows=ms_rows,
                       row_blk0=mp // 256).reshape(bs, ls, d_model)

    tail = lambda v, n, width: jnp.stack([v[(b + 1) * lp - n:(b + 1) * lp, :width] for b in range(bp)])
    u1_s = u1[mp:].reshape(bs, ls, d_model)
    new_pool_s = jnp.concatenate([state_pool[0], u1_s], axis=1)[:, -POOL_HIST:]
    new_conv_s = jnp.concatenate([state_ssm_conv[li], proj[mp:, :CONV_DIM].reshape(bs, ls, CONV_DIM)],
                                 axis=1)[:, -(CONV_W - 1):]
    st = lambda h, nb: h.reshape(nb, SSM_HEADS, SSM_HEAD_DIM, SSM_D_STATE)
    return (y_prompt, y_sample,
            ckv[:mp].reshape(1, bp, lp, KV_LORA), kpe[:mp, :QK_ROPE].reshape(1, bp, lp, QK_ROPE),
            ckv[mp:].reshape(1, bs, ls, KV_LORA), kpe[mp:, :QK_ROPE].reshape(1, bs, ls, QK_ROPE),
            tail(proj, CONV_W - 1, CONV_DIM)[None], new_conv_s[None], st(h_p, bp)[None], st(h_s, bs)[None],
            tail(u1, POOL_HIST, d_model)[None], new_pool_s[None])
```
